```python
import math
import jax, jax.numpy as jnp
from jax import lax
import numpy as np

D_MODEL = 2048
BATCH = 2
SEQ = 8192
DEPTH = 2

CHUNK = 64
N_META = 16
N_A_LAYERS = DEPTH // 2
N_B_LAYERS = DEPTH - N_A_LAYERS
SSD_EXPAND = 2
SSD_D_INNER = SSD_EXPAND * D_MODEL
SSD_HEADDIM = 64
SSD_HEADS = SSD_D_INNER // SSD_HEADDIM
SSD_GROUPS = 8
SSD_HEADS_PER_GROUP = SSD_HEADS // SSD_GROUPS
SSD_STATE = 128
SSD_CONV = 4
SSD_CONV_DIM = SSD_D_INNER + 2 * SSD_GROUPS * SSD_STATE
SSD_IN_DIM = SSD_D_INNER + SSD_CONV_DIM + SSD_HEADS
SB_HEAD_DIM = 128
SB_HEADS = D_MODEL // SB_HEAD_DIM
SB_WIDTH = SB_HEADS * SB_HEAD_DIM
Q_BLOCK = 128
FFN_HIDDEN = -(-8 * D_MODEL // (3 * 256)) * 256
EPS = 1e-6

kernel_name = "yoco_ssd_stickbreaking_hybrid"


def rmsnorm(x, g):
    xf = x.astype(jnp.float32)
    y = xf * lax.rsqrt(jnp.mean(xf * xf, axis=-1, keepdims=True) + EPS)
    return (y * g.astype(jnp.float32)).astype(x.dtype)


def causal_depthwise_conv(x, w, b):
    k, c = w.shape
    out = lax.conv_general_dilated(
        x, w[:, None, :].astype(x.dtype), window_strides=(1,), padding=((k - 1, 0),),
        dimension_numbers=("NWC", "WIO", "NWC"), feature_group_count=c)
    return out + b.astype(x.dtype)


def ssd_chunked_scan(xs, dt, A, Bm, Cm):
    b, t = xs.shape[:2]
    nc = t // CHUNK

    def to_chunks(a):
        return a.reshape((b, nc, CHUNK) + a.shape[2:]).swapaxes(0, 1)

    idx = jnp.arange(CHUNK)
    causal = (idx[:, None] >= idx[None, :])[None, :, :, None, None]

    def step(state, inp):
        xc, dtc, bc, cc = inp
        cum = jnp.cumsum(dtc * A, axis=1)
        seg = cum[:, :, None] - cum[:, None, :]
        decay = jnp.exp(jnp.where(causal, seg, -jnp.inf))
        cb = jnp.einsum("bqgn,bsgn->bqsg", cc, bc)
        xdt = xc * dtc[..., None]
        y_diag = jnp.einsum("bqsg,bqsgr,bsgrp->bqgrp", cb, decay, xdt)
        y_off = jnp.einsum("bqgn,bgrpn->bqgrp", cc, state) * jnp.exp(cum)[..., None]
        last = cum[:, -1]
        w = jnp.exp(last[:, None] - cum) * dtc
        new_state = state * jnp.exp(last)[..., None, None] + jnp.einsum(
            "bsgn,bsgr,bsgrp->bgrpn", bc, w, xc)
        return new_state, y_diag + y_off

    state0 = jnp.zeros((b, SSD_GROUPS, SSD_HEADS_PER_GROUP, SSD_HEADDIM, SSD_STATE), jnp.float32)
    _, ys = lax.scan(step, state0, (to_chunks(xs), to_chunks(dt), to_chunks(Bm), to_chunks(Cm)))
    return ys.swapaxes(0, 1).reshape(xs.shape)


def ssd_mixer(u, in_proj, conv_w, conv_b, dt_bias, a_log, d_skip, norm_g, out_proj):
    b, l, _ = u.shape
    zxbcdt = u @ in_proj
    z = zxbcdt[..., :SSD_D_INNER]
    xbc = zxbcdt[..., SSD_D_INNER:SSD_D_INNER + SSD_CONV_DIM]
    dt = zxbcdt[..., SSD_D_INNER + SSD_CONV_DIM:]
    xbc = jax.nn.silu(causal_depthwise_conv(xbc, conv_w, conv_b)).astype(jnp.float32)
    dt = jax.nn.softplus(dt.astype(jnp.float32) + dt_bias.astype(jnp.float32))
    pad = CHUNK - N_META
    xbc = jnp.pad(xbc, ((0, 0), (pad, 0), (0, 0)))
    dt = jnp.pad(dt, ((0, 0), (pad, 0), (0, 0)))
    t = l + pad
    gn = SSD_GROUPS * SSD_STATE
    xs = xbc[..., :SSD_D_INNER].reshape(b, t, SSD_GROUPS, SSD_HEADS_PER_GROUP, SSD_HEADDIM)
    bm = xbc[..., SSD_D_INNER:SSD_D_INNER + gn].reshape(b, t, SSD_GROUPS, SSD_STATE)
    cm = xbc[..., SSD_D_INNER + gn:].reshape(b, t, SSD_GROUPS, SSD_STATE)
    dt = dt.reshape(b, t, SSD_GROUPS, SSD_HEADS_PER_GROUP)
    a = -jnp.exp(a_log.astype(jnp.float32)).reshape(SSD_GROUPS, SSD_HEADS_PER_GROUP)
    y = ssd_chunked_scan(xs, dt, a, bm, cm)
    y = y + xs * d_skip.astype(jnp.float32).reshape(SSD_GROUPS, SSD_HEADS_PER_GROUP)[..., None]
    y = y[:, pad:].reshape(b, l, SSD_D_INNER)
    g = (y * jax.nn.silu(z.astype(jnp.float32))).reshape(b, l, SSD_GROUPS, SSD_D_INNER // SSD_GROUPS)
    g = g * lax.rsqrt(jnp.mean(g * g, axis=-1, keepdims=True) + EPS)
    g = g.reshape(b, l, SSD_D_INNER) * norm_g.astype(jnp.float32)
    return g.astype(u.dtype) @ out_proj


def stick_breaking_attention(u, w_q, k, v, w_o):
    b, l, _ = u.shape
    q = (u @ w_q).reshape(b, l, SB_HEADS, SB_HEAD_DIM)
    nb = -(-l // Q_BLOCK)
    lp = nb * Q_BLOCK
    q = jnp.pad(q, ((0, 0), (0, lp - l), (0, 0), (0, 0)))
    qb = q.reshape(b, nb, Q_BLOCK, SB_HEADS, SB_HEAD_DIM).swapaxes(0, 1)
    starts = jnp.arange(nb) * Q_BLOCK
    kpos = jnp.arange(l)
    scale = SB_HEAD_DIM ** -0.5

    def block(args):
        qblk, start = args
        qpos = start + jnp.arange(Q_BLOCK)
        z = jnp.einsum("bqhd,bkhd->bhqk", qblk, k).astype(jnp.float32) * scale
        before = (kpos[None, :] < qpos[:, None])[None, None]
        log_keep = jnp.where(before, jax.nn.log_sigmoid(-z), 0.0)
        tail = lax.cumsum(log_keep, axis=3, reverse=True) - log_keep
        att = jnp.where(before, jnp.exp(jax.nn.log_sigmoid(z) + tail), 0.0)
        return jnp.einsum("bhqk,bkhd->bqhd", att.astype(v.dtype), v)

    o = lax.map(block, (qb, starts))
    o = o.swapaxes(0, 1).reshape(b, lp, SB_WIDTH)[:, :l]
    return o @ w_o


def swiglu(u, w_gate_up, w_down):
    gu = u @ w_gate_up
    return (jax.nn.silu(gu[..., :FFN_HIDDEN]) * gu[..., FFN_HIDDEN:]) @ w_down


def setup_inputs(seed: int = 0) -> dict:
    key = jax.random.key(seed)
    ks = jax.random.split(key, 24)
    f32 = jnp.float32

    def nrm(k, shape, scale):
        return jax.random.normal(k, shape, f32) * scale

    x = nrm(ks[0], (BATCH, SEQ, D_MODEL), 1.0)
    meta_tokens = nrm(ks[1], (N_META, D_MODEL), 1.0)
    norm_mix = 1.0 + nrm(ks[2], (DEPTH, D_MODEL), 0.02)
    norm_ffn = 1.0 + nrm(ks[3], (DEPTH, D_MODEL), 0.02)
    ssd_in_proj = nrm(ks[4], (N_A_LAYERS, D_MODEL, SSD_IN_DIM), D_MODEL ** -0.5)
    ssd_conv_w = nrm(ks[5], (N_A_LAYERS, SSD_CONV, SSD_CONV_DIM), SSD_CONV ** -0.5)
    ssd_conv_b = nrm(ks[6], (N_A_LAYERS, SSD_CONV_DIM), 0.02)
    dt0 = jnp.exp(jax.random.uniform(ks[7], (N_A_LAYERS, SSD_HEADS), f32,
                                     math.log(1e-3), math.log(1e-1)))
    ssd_dt_bias = dt0 + jnp.log(-jnp.expm1(-dt0))
    ssd_a_log = jnp.log(jax.random.uniform(ks[8], (N_A_LAYERS, SSD_HEADS), f32, 1.0, 16.0))
    ssd_d = 1.0 + nrm(ks[9], (N_A_LAYERS, SSD_HEADS), 0.02)
    ssd_norm = 1.0 + nrm(ks[10], (N_A_LAYERS, SSD_D_INNER), 0.02)
    ssd_out_proj = nrm(ks[11], (N_A_LAYERS, SSD_D_INNER, D_MODEL), SSD_D_INNER ** -0.5)
    kv_norm = 1.0 + nrm(ks[12], (D_MODEL,), 0.02)
    w_kv = nrm(ks[13], (D_MODEL, 2 * SB_WIDTH), D_MODEL ** -0.5)
    sb_w_q = nrm(ks[14], (N_B_LAYERS, D_MODEL, SB_WIDTH), D_MODEL ** -0.5)
    sb_w_o = nrm(ks[15], (N_B_LAYERS, SB_WIDTH, D_MODEL), SB_WIDTH ** -0.5)
    ffn_gate_up = nrm(ks[16], (DEPTH, D_MODEL, 2 * FFN_HIDDEN), D_MODEL ** -0.5)
    ffn_down = nrm(ks[17], (DEPTH, FFN_HIDDEN, D_MODEL), FFN_HIDDEN ** -0.5)
    final_norm = 1.0 + nrm(ks[18], (D_MODEL,), 0.02)
    return {"x": x, "meta_tokens": meta_tokens, "norm_mix": norm_mix, "norm_ffn": norm_ffn,
            "ssd_in_proj": ssd_in_proj, "ssd_conv_w": ssd_conv_w, "ssd_conv_b": ssd_conv_b,
            "ssd_dt_bias": ssd_dt_bias, "ssd_a_log": ssd_a_log, "ssd_d": ssd_d,
            "ssd_norm": ssd_norm, "ssd_out_proj": ssd_out_proj, "kv_norm": kv_norm,
            "w_kv": w_kv, "sb_w_q": sb_w_q, "sb_w_o": sb_w_o, "ffn_gate_up": ffn_gate_up,
            "ffn_down": ffn_down, "final_norm": final_norm}


def reference(x, meta_tokens, norm_mix, norm_ffn, ssd_in_proj, ssd_conv_w, ssd_conv_b,
              ssd_dt_bias, ssd_a_log, ssd_d, ssd_norm, ssd_out_proj, kv_norm, w_kv,
              sb_w_q, sb_w_o, ffn_gate_up, ffn_down, final_norm):
    b = x.shape[0]
    meta = jnp.broadcast_to(meta_tokens[None].astype(x.dtype), (b, N_META, D_MODEL))
    h = jnp.concatenate([meta, x], axis=1)
    l = h.shape[1]
    k_shared = None
    v_shared = None
    for layer in range(DEPTH):
        u = rmsnorm(h, norm_mix[layer])
        if layer < N_A_LAYERS:
            i = layer
            h = h + ssd_mixer(u, ssd_in_proj[i], ssd_conv_w[i], ssd_conv_b[i], ssd_dt_bias[i],
                              ssd_a_log[i], ssd_d[i], ssd_norm[i], ssd_out_proj[i])
        else:
            j = layer - N_A_LAYERS
            h = h + stick_breaking_attention(u, sb_w_q[j], k_shared, v_shared, sb_w_o[j])
        h = h + swiglu(rmsnorm(h, norm_ffn[layer]), ffn_gate_up[layer], ffn_down[layer])
        if layer == N_A_LAYERS - 1:
            kv = rmsnorm(h, kv_norm) @ w_kv
            k_shared = kv[..., :SB_WIDTH].reshape(b, l, SB_HEADS, SB_HEAD_DIM)
            v_shared = kv[..., SB_WIDTH:].reshape(b, l, SB_HEADS, SB_HEAD_DIM)
    return rmsnorm(h, final_norm)[:, N_META:]
```

```python
import functools

import jax
import jax.numpy as jnp
from jax import lax
from jax.experimental import pallas as pl
from jax.experimental.pallas import tpu as pltpu

F32 = jnp.float32
BF16 = jnp.bfloat16
EPS = 1e-6
CHUNK = 64
N_META = 16
SSD_HEADDIM = 64
SSD_GROUPS = 8
SSD_STATE = 128
SSD_CONV = 4
SB_HEAD_DIM = 128
LANES = 128
ATTN_BLOCK = 256
EXP_ZERO_F32 = -104.0
MIB = 1024 * 1024
HIGHEST = lax.Precision.HIGHEST


def _params(sem, vmem_mib):
    return pltpu.CompilerParams(dimension_semantics=sem, vmem_limit_bytes=vmem_mib * MIB)


def _dot(a, b):
    return jnp.dot(a, b, preferred_element_type=F32)


def _dot_nt(a, b):
    return lax.dot_general(a, b, (((1,), (1,)), ((), ())), preferred_element_type=F32)


def _dot_tn(a, b, precision=None):
    return lax.dot_general(a, b, (((0,), (0,)), ((), ())), precision=precision,
                           preferred_element_type=F32)


def _sigmoid(x):
    return 1.0 / (1.0 + jnp.exp(-x))


def _store_normed(x_ref, g_ref, u_ref):
    rows = x_ref.shape[0]
    slab = min(rows, 256)
    for r in range(0, rows, slab):
        x = x_ref[r:r + slab, :]
        ms = jnp.mean(x * x, axis=-1, keepdims=True)
        u_ref[r:r + slab, :] = (x * lax.rsqrt(ms + EPS) * g_ref[...]).astype(BF16)


def _in_proj_kernel(x_ref, g_ref, w_ref, wdt_ref, o_ref, odt_ref, u_ref):
    @pl.when(pl.program_id(1) == 0)
    def _():
        _store_normed(x_ref, g_ref, u_ref)
        odt_ref[...] = _dot(u_ref[...], wdt_ref[...])

    o_ref[...] = _dot(u_ref[...], w_ref[...])


def _in_proj(x2d, gain, w_zx, w_dt, tm):
    m, k = x2d.shape
    n = w_zx.shape[1]
    tn = 512
    return pl.pallas_call(
        _in_proj_kernel,
        grid=(m // tm, n // tn),
        in_specs=[
            pl.BlockSpec((tm, k), lambda i, j: (i, 0)),
            pl.BlockSpec((1, k), lambda i, j: (0, 0)),
            pl.BlockSpec((k, tn), lambda i, j: (0, j)),
            pl.BlockSpec((k, LANES), lambda i, j: (0, 0)),
        ],
        out_specs=[
            pl.BlockSpec((tm, tn), lambda i, j: (i, j)),
            pl.BlockSpec((tm, LANES), lambda i, j: (i, 0)),
        ],
        out_shape=[jax.ShapeDtypeStruct((m, n), F32), jax.ShapeDtypeStruct((m, LANES), F32)],
        scratch_shapes=[pltpu.VMEM((tm, k), BF16)],
        compiler_params=_params(("parallel", "arbitrary"), 48),
        name="in_proj",
    )(x2d, gain, w_zx, w_dt)


def _ssd_kernel(z0_ref, z1_ref, xa_ref, xb_ref, bc_ref, dt_ref, cw_ref, cb_ref, dtb_ref,
                alog_ref, dsk_ref, ng_ref, st0_ref, car0_ref,
                g_ref, stout_ref, state_ref, xext_ref, *, first_valid_row):
    c = pl.program_id(1)
    q_len = CHUNK
    d_inner = g_ref.shape[1]
    gw = d_inner // SSD_GROUPS
    half = xa_ref.shape[1]
    bc_off = d_inner
    cc_off = d_inner + SSD_GROUPS * SSD_STATE

    @pl.when(c == 0)
    def _():
        state_ref[...] = st0_ref[...]
        xext_ref[0:8, :] = car0_ref[...]

    xext_ref[8:8 + q_len, 0:half] = xa_ref[...]
    xext_ref[8:8 + q_len, half:2 * half] = xb_ref[...]
    xext_ref[8:8 + q_len, 2 * half:3 * half] = bc_ref[...]

    row = lax.broadcasted_iota(jnp.int32, (q_len, 1), 0)
    valid = row >= first_valid_row

    def conv_act(lo, width):
        acc = cb_ref[:, lo:lo + width]
        for k in range(SSD_CONV):
            s = 8 - (SSD_CONV - 1) + k
            acc = acc + cw_ref[k:k + 1, lo:lo + width] * xext_ref[s:s + q_len, lo:lo + width]
        a = acc * _sigmoid(acc)
        if first_valid_row:
            a = jnp.where(valid, a, 0.0)
        return a

    nh = dtb_ref.shape[1]
    dtr = dt_ref[:, 0:nh] + dtb_ref[...]
    dt = jnp.maximum(dtr, 0.0) + jnp.log1p(jnp.exp(-jnp.abs(dtr)))
    if first_valid_row:
        dt = jnp.where(valid, dt, 0.0)
    dta = dt * (-jnp.exp(alog_ref[...]))

    ri = lax.broadcasted_iota(jnp.int32, (q_len, q_len), 0)
    ci = lax.broadcasted_iota(jnp.int32, (q_len, q_len), 1)
    tri_incl = (ri >= ci).astype(F32)
    eye = (ri == ci).astype(F32)
    cum = jnp.dot(tri_incl, dta, precision=HIGHEST, preferred_element_type=F32)
    cum_t = _dot_tn(dta, (ri <= ci).astype(F32), precision=HIGHEST)
    dt_t = _dot_tn(dt, eye, precision=HIGHEST)
    w_in = jnp.exp(cum[q_len - 1:q_len, :] - cum) * dt

    lane = lax.broadcasted_iota(jnp.int32, (q_len, LANES), 1)
    rowq = lax.broadcasted_iota(jnp.int32, (q_len, LANES), 0)
    left = lane < SSD_HEADDIM
    causal2 = rowq >= jnp.where(left, lane, lane - SSD_HEADDIM)
    rr = lax.broadcasted_iota(jnp.int32, (2 * q_len, LANES), 0)
    ll = lax.broadcasted_iota(jnp.int32, (2 * q_len, LANES), 1)
    blockdiag = (rr < q_len) == (ll < SSD_HEADDIM)

    def pair_cols(a, h):
        lo = jnp.broadcast_to(a[:, h:h + 1], (q_len, LANES))
        hi = jnp.broadcast_to(a[:, h + 1:h + 2], (q_len, LANES))
        return jnp.where(left, lo, hi)

    def pair_rows(a, h):
        return jnp.concatenate([a[h:h + 1, :], a[h + 1:h + 2, :]], axis=1)

    heads_per_group = gw // SSD_HEADDIM
    for g in range(SSD_GROUPS):
        xs = conv_act(g * gw, gw)
        bm = conv_act(bc_off + g * SSD_STATE, SSD_STATE)
        cm = conv_act(cc_off + g * SSD_STATE, SSD_STATE)
        bm16 = bm.astype(BF16)
        cm16 = cm.astype(BF16)
        cb2 = _dot_nt(cm16, jnp.concatenate([bm16, bm16], axis=0))
        st = state_ref[g]
        y_off = _dot(cm16, st.astype(BF16))
        ys = []
        xws = []
        decs = []
        for p in range(heads_per_group // 2):
            h = g * heads_per_group + 2 * p
            sl = slice(p * LANES, (p + 1) * LANES)
            xs2 = xs[:, sl]
            a2 = pair_cols(cum, h)
            seg = a2 - pair_rows(cum_t, h)
            m2 = cb2 * jnp.exp(jnp.where(causal2, seg, -jnp.inf)) * pair_rows(dt_t, h)
            xbd = jnp.where(blockdiag, jnp.concatenate([xs2, xs2], axis=0), 0.0).astype(BF16)
            ea2 = jnp.exp(a2)
            y2 = _dot(m2.astype(BF16), xbd) + y_off[:, sl] * ea2 + xs2 * dsk_ref[:, g * gw + p * LANES:g * gw + (p + 1) * LANES]
            ys.append(y2)
            xws.append((xs2 * pair_cols(w_in, h)).astype(BF16))
            decs.append(ea2[q_len - 1:q_len, :])
        xw = jnp.concatenate(xws, axis=1)
        dec = jnp.concatenate(decs, axis=1)
        state_ref[g] = st * dec + _dot_tn(bm16, xw)
        y = jnp.concatenate(ys, axis=1)
        zr = z0_ref if (g * gw) < half else z1_ref
        zc = (g * gw) % half
        zz = zr[:, zc:zc + gw]
        gt = y * (zz * _sigmoid(zz))
        ms = jnp.mean(gt * gt, axis=-1, keepdims=True)
        g_ref[:, g * gw:(g + 1) * gw] = (gt * lax.rsqrt(ms + EPS) * ng_ref[:, g * gw:(g + 1) * gw]).astype(g_ref.dtype)

    xext_ref[0:8, :] = xext_ref[q_len:q_len + 8, :]

    @pl.when(c == pl.num_programs(1) - 1)
    def _():
        stout_ref[...] = state_ref[...]


def _ssd(zx, dtraw, conv_w, conv_b, dt_bias, a_log, d_exp, norm_g, st0, car0, *, nb, nck,
         first_valid_row):
    d_inner = norm_g.shape[1]
    conv_dim = conv_w.shape[1]
    half = (conv_dim - d_inner)
    assert d_inner == 2 * half and zx.shape[1] == d_inner + conv_dim
    gw = d_inner // SSD_GROUPS
    nh = dt_bias.shape[1]

    def rows(col):
        return pl.BlockSpec((CHUNK, half), lambda b, c: (b * nck + c, col))

    def full(a):
        nd = a.ndim
        return pl.BlockSpec(a.shape, lambda b, c: (0,) * nd)

    kern = functools.partial(_ssd_kernel, first_valid_row=first_valid_row)
    return pl.pallas_call(
        kern,
        grid=(nb, nck),
        in_specs=[rows(0), rows(1), rows(2), rows(3), rows(4),
                  pl.BlockSpec((CHUNK, LANES), lambda b, c: (b * nck + c, 0)),
                  full(conv_w), full(conv_b), full(dt_bias), full(a_log), full(d_exp), full(norm_g),
                  full(st0), full(car0)],
        out_specs=[pl.BlockSpec((CHUNK, d_inner), lambda b, c: (b * nck + c, 0)),
                   pl.BlockSpec((None, SSD_GROUPS, SSD_STATE, gw), lambda b, c: (b, 0, 0, 0))],
        out_shape=[jax.ShapeDtypeStruct((nb * nck * CHUNK, d_inner), BF16),
                   jax.ShapeDtypeStruct((nb, SSD_GROUPS, SSD_STATE, gw), F32)],
        scratch_shapes=[pltpu.VMEM((SSD_GROUPS, SSD_STATE, gw), F32),
                        pltpu.VMEM((8 + CHUNK, conv_dim), F32)],
        compiler_params=_params(("parallel", "arbitrary"), 32),
        name="ssd_scan",
    )(zx, zx, zx, zx, zx, dtraw, conv_w, conv_b, dt_bias, a_log, d_exp, norm_g, st0, car0)


def _mm_res_kernel(a_ref, w_ref, h_ref, o_ref):
    o_ref[...] = h_ref[...] + _dot(a_ref[...], w_ref[...])


def _mm_res(a, w, h, tm, name):
    m, k = a.shape
    n = w.shape[1]
    tn = 512
    return pl.pallas_call(
        _mm_res_kernel,
        grid=(m // tm, n // tn),
        in_specs=[pl.BlockSpec((tm, k), lambda i, j: (i, 0)),
                  pl.BlockSpec((k, tn), lambda i, j: (0, j)),
                  pl.BlockSpec((tm, tn), lambda i, j: (i, j))],
        out_specs=pl.BlockSpec((tm, tn), lambda i, j: (i, j)),
        out_shape=jax.ShapeDtypeStruct((m, n), F32),
        compiler_params=_params(("parallel", "arbitrary"), 48),
        name=name,
    )(a, w, h)


def _ffn_kernel(h_ref, g_ref, wg_ref, wu_ref, wd_ref, fg_ref, o_ref, u_ref, *, final_norm):
    j = pl.program_id(1)

    @pl.when(j == 0)
    def _():
        _store_normed(h_ref, g_ref, u_ref)
        o_ref[...] = h_ref[...]

    u = u_ref[...]
    gate = _dot(u, wg_ref[...])
    up = _dot(u, wu_ref[...])
    act = (gate * _sigmoid(gate) * up).astype(BF16)
    o_ref[...] += _dot(act, wd_ref[...])

    if final_norm:
        @pl.when(j == pl.num_programs(1) - 1)
        def _():
            rows = o_ref.shape[0]
            slab = min(rows, 256)
            for r in range(0, rows, slab):
                x = o_ref[r:r + slab, :]
                ms = jnp.mean(x * x, axis=-1, keepdims=True)
                o_ref[r:r + slab, :] = x * lax.rsqrt(ms + EPS) * fg_ref[...]


def _ffn(h, gain, w_gate_up, w_down, final_gain, tm, final_norm, name):
    m, d = h.shape
    f = w_down.shape[0]
    tf = 512
    nf = f // tf
    kern = functools.partial(_ffn_kernel, final_norm=final_norm)
    return pl.pallas_call(
        kern,
        grid=(m // tm, nf),
        in_specs=[pl.BlockSpec((tm, d), lambda i, j: (i, 0)),
                  pl.BlockSpec((1, d), lambda i, j: (0, 0)),
                  pl.BlockSpec((d, tf), lambda i, j: (0, j)),
                  pl.BlockSpec((d, tf), lambda i, j: (0, nf + j)),
                  pl.BlockSpec((tf, d), lambda i, j: (j, 0)),
                  pl.BlockSpec((1, d), lambda i, j: (0, 0))],
        out_specs=pl.BlockSpec((tm, d), lambda i, j: (i, 0)),
        out_shape=jax.ShapeDtypeStruct((m, d), F32),
        scratch_shapes=[pltpu.VMEM((tm, d), BF16)],
        compiler_params=_params(("parallel", "arbitrary"), 48),
        name=name,
    )(h, gain, w_gate_up, w_gate_up, w_down, final_gain)


def _qkv_kernel(h_ref, gq_ref, gkv_ref, w_ref, o_ref, uq_ref, ukv_ref, *, nq_tiles):
    j = pl.program_id(1)

    @pl.when(j == 0)
    def _():
        _store_normed(h_ref, gq_ref, uq_ref)
        _store_normed(h_ref, gkv_ref, ukv_ref)

    @pl.when(j < nq_tiles)
    def _():
        o_ref[...] = _dot(uq_ref[...], w_ref[...]).astype(o_ref.dtype)

    @pl.when(j >= nq_tiles)
    def _():
        o_ref[...] = _dot(ukv_ref[...], w_ref[...]).astype(o_ref.dtype)


def _qkv(h, gq, gkv, w_qkv, nq_cols, tm):
    m, d = h.shape
    n = w_qkv.shape[1]
    tn = 512
    kern = functools.partial(_qkv_kernel, nq_tiles=nq_cols // tn)
    return pl.pallas_call(
        kern,
        grid=(m // tm, n // tn),
        in_specs=[pl.BlockSpec((tm, d), lambda i, j: (i, 0)),
                  pl.BlockSpec((1, d), lambda i, j: (0, 0)),
                  pl.BlockSpec((1, d), lambda i, j: (0, 0)),
                  pl.BlockSpec((d, tn), lambda i, j: (0, j))],
        out_specs=pl.BlockSpec((tm, tn), lambda i, j: (i, j)),
        out_shape=jax.ShapeDtypeStruct((m, n), BF16),
        scratch_shapes=[pltpu.VMEM((tm, d), BF16), pltpu.VMEM((tm, d), BF16)],
        compiler_params=_params(("parallel", "arbitrary"), 48),
        name="qkv_proj",
    )(h, gq, gkv, w_qkv)


def _attn_kernel(q_ref, k_ref, v_ref, km_ref, vm_ref, o_ref, acc_ref, c_ref, *, scale):
    t = q_ref.shape[0]
    qi = pl.program_id(2)
    q = q_ref[...]
    ri = lax.broadcasted_iota(jnp.int32, (t, t), 0)
    ci = lax.broadcasted_iota(jnp.int32, (t, t), 1)
    upper = (ri > ci).astype(BF16)

    def step(kblk, vblk, mask, u):
        z = _dot_nt(q, kblk) * scale
        ls = jnp.minimum(z, 0.0) - jnp.log1p(jnp.exp(-jnp.abs(z)))
        lk = ls - z
        if mask is not None:
            lk = jnp.where(mask, lk, 0.0)
        hi = lk.astype(BF16)
        lo = (lk - hi.astype(F32)).astype(BF16)
        c = c_ref[...]
        p = jnp.exp(ls + (_dot(hi, u) + _dot(lo, u)) + c)
        if mask is not None:
            p = jnp.where(mask, p, 0.0)
        acc_ref[...] += _dot(p.astype(BF16), vblk)
        c_new = c + jnp.sum(lk, axis=1, keepdims=True)
        c_ref[...] = c_new
        return (jnp.max(c_new) > EXP_ZERO_F32).astype(jnp.int32)

    acc_ref[...] = jnp.zeros_like(acc_ref)
    c_ref[...] = jnp.zeros_like(c_ref)
    start = pl.multiple_of(qi * t, t)
    go = step(k_ref[pl.ds(start, t), :], v_ref[pl.ds(start, t), :], ci < ri, upper)

    def cond(carry):
        kb, alive = carry
        return jnp.logical_and(kb >= 0, alive > 0)

    def body(carry):
        kb, _ = carry
        s = pl.multiple_of(kb * t, t)
        alive = step(k_ref[pl.ds(s, t), :], v_ref[pl.ds(s, t), :], None, upper)
        return kb - 1, alive

    _, go = lax.while_loop(cond, body, (qi - 1, go))

    @pl.when(go > 0)
    def _():
        tm = km_ref.shape[0]
        mi = lax.broadcasted_iota(jnp.int32, (t, tm), 1)
        mr = lax.broadcasted_iota(jnp.int32, (tm, tm), 0)
        mc = lax.broadcasted_iota(jnp.int32, (tm, tm), 1)
        step(km_ref[...], vm_ref[...], mi < N_META, (mr > mc).astype(BF16))

    o_ref[...] = acc_ref[...].astype(o_ref.dtype)


def _attention(qkv, kv_meta, nb, seq, n_heads):
    t = ATTN_BLOCK
    hd = SB_HEAD_DIM
    kern = functools.partial(_attn_kernel, scale=hd ** -0.5)
    qkv3 = qkv.reshape(nb, seq, qkv.shape[1])
    return pl.pallas_call(
        kern,
        grid=(nb, n_heads, seq // t),
        in_specs=[pl.BlockSpec((None, t, hd), lambda b, h, i: (b, i, h)),
                  pl.BlockSpec((None, seq, hd), lambda b, h, i: (b, 0, n_heads + h)),
                  pl.BlockSpec((None, seq, hd), lambda b, h, i: (b, 0, 2 * n_heads + h)),
                  pl.BlockSpec((LANES, hd), lambda b, h, i: (0, n_heads + h)),
                  pl.BlockSpec((LANES, hd), lambda b, h, i: (0, 2 * n_heads + h))],
        out_specs=pl.BlockSpec((None, t, hd), lambda b, h, i: (b, i, h)),
        out_shape=jax.ShapeDtypeStruct((nb, seq, n_heads * hd), BF16),
        scratch_shapes=[pltpu.VMEM((t, hd), F32), pltpu.VMEM((t, 1), F32)],
        compiler_params=_params(("parallel", "parallel", "arbitrary"), 32),
        name="sb_attention",
    )(qkv3, qkv3, qkv3, kv_meta, kv_meta)


def _row_tile(m):
    return 1024 if m % 1024 == 0 else m


def kernel(x, meta_tokens, norm_mix, norm_ffn, ssd_in_proj, ssd_conv_w, ssd_conv_b, ssd_dt_bias,
           ssd_a_log, ssd_d, ssd_norm, ssd_out_proj, kv_norm, w_kv, sb_w_q, sb_w_o, ffn_gate_up,
           ffn_down, final_norm):
    nb, seq, d = x.shape
    d_inner = ssd_out_proj.shape[1]
    conv_dim = ssd_conv_w.shape[2]
    n_ssd_heads = ssd_dt_bias.shape[1]
    sb_width = sb_w_q.shape[2]
    n_sb_heads = sb_width // SB_HEAD_DIM
    assert seq % ATTN_BLOCK == 0 and seq % CHUNK == 0 and n_ssd_heads <= LANES

    w_in = ssd_in_proj[0]
    w_zx = w_in[:, :d_inner + conv_dim].astype(BF16)
    w_dt = jnp.pad(w_in[:, d_inner + conv_dim:], ((0, 0), (0, LANES - n_ssd_heads))).astype(BF16)
    w_out = ssd_out_proj[0].astype(BF16)
    w_qkv = jnp.concatenate([sb_w_q[0], w_kv], axis=1).astype(BF16)
    w_o = sb_w_o[0].astype(BF16)
    w_gu = [ffn_gate_up[l].astype(BF16) for l in range(2)]
    w_dn = [ffn_down[l].astype(BF16) for l in range(2)]
    row = lambda v: v.reshape(1, -1).astype(F32)
    d_exp = row(jnp.repeat(ssd_d[0], SSD_HEADDIM))
    ssd_consts = (ssd_conv_w[0].astype(F32), row(ssd_conv_b[0]), row(ssd_dt_bias[0]),
                  row(ssd_a_log[0]), d_exp, row(ssd_norm[0]))
    gw = d_inner // SSD_GROUPS

    hm = meta_tokens.astype(F32)
    zx_m, dt_m = _in_proj(hm, row(norm_mix[0]), w_zx, w_dt, N_META)
    lead = CHUNK - N_META
    g_m, st_m = _ssd(jnp.pad(zx_m, ((lead, 0), (0, 0))), jnp.pad(dt_m, ((lead, 0), (0, 0))),
                     *ssd_consts, jnp.zeros((SSD_GROUPS, SSD_STATE, gw), F32),
                     jnp.zeros((8, conv_dim), F32), nb=1, nck=1, first_valid_row=lead)
    hm = _mm_res(g_m[lead:], w_out, hm, N_META, "out_proj_meta")
    hm = _ffn(hm, row(norm_ffn[0]), w_gu[0], w_dn[0], row(final_norm), N_META, False, "ffn0_meta")
    qkv_m = _qkv(hm, row(norm_mix[1]), row(kv_norm), w_qkv, sb_width, N_META)
    kv_meta = jnp.pad(qkv_m, ((0, LANES - N_META), (0, 0)))

    tm = _row_tile(nb * seq)
    h = x.reshape(nb * seq, d).astype(F32)
    zx, dtr = _in_proj(h, row(norm_mix[0]), w_zx, w_dt, tm)
    car0 = zx_m[N_META - 8:, d_inner:]
    g, _ = _ssd(zx, dtr, *ssd_consts, st_m[0], car0, nb=nb, nck=seq // CHUNK, first_valid_row=0)
    h = _mm_res(g, w_out, h, tm, "out_proj")
    h = _ffn(h, row(norm_ffn[0]), w_gu[0], w_dn[0], row(final_norm), tm // 2, False, "ffn0")

    qkv = _qkv(h, row(norm_mix[1]), row(kv_norm), w_qkv, sb_width, tm)
    o = _attention(qkv, kv_meta, nb, seq, n_sb_heads)
    h = _mm_res(o.reshape(nb * seq, sb_width), w_o, h, tm, "o_proj")
    out = _ffn(h, row(norm_ffn[1]), w_gu[1], w_dn[1], row(final_norm), tm // 2, True, "ffn1")
    return out.reshape(nb, seq, d)
```

```python
import functools

import jax
import jax.numpy as jnp
from jax import lax
from jax.experimental import pallas as pl
from jax.experimental.pallas import tpu as pltpu

F32 = jnp.float32
BF16 = jnp.bfloat16
EPS = 1e-6
CHUNK = 64
N_META = 16
HALO = 16
SSD_HEADDIM = 64
SSD_GROUPS = 8
SSD_STATE = 128
SSD_CONV = 4
SSD_GROUP_BATCH = 4
SB_HEAD_DIM = 128
LANES = 128
ATTN_BLOCK = 256
ATTN_HEADS_PER_STEP = 4
LOG2E = 1.4426950408889634
EXP_ZERO_F32 = -104.0
MIB = 1024 * 1024
HIGHEST = lax.Precision.HIGHEST


def _params(sem, vmem_mib):
    return pltpu.CompilerParams(dimension_semantics=sem, vmem_limit_bytes=vmem_mib * MIB)


def _dot(a, b):
    return jnp.dot(a, b, preferred_element_type=F32)


def _dot_nt(a, b):
    return lax.dot_general(a, b, (((1,), (1,)), ((), ())), preferred_element_type=F32)


def _dot_tn(a, b, precision=None):
    return lax.dot_general(a, b, (((0,), (0,)), ((), ())), precision=precision,
                           preferred_element_type=F32)


def _silu(x):
    hx = 0.5 * x
    return hx + hx * jnp.tanh(hx)


def _store_normed(x_ref, g_ref, u_ref):
    rows = x_ref.shape[0]
    slab = min(rows, 256)
    for r in range(0, rows, slab):
        x = x_ref[r:r + slab, :]
        ms = jnp.mean(x * x, axis=-1, keepdims=True)
        u_ref[r:r + slab, :] = (x * lax.rsqrt(ms + EPS) * g_ref[...]).astype(BF16)


def _in_proj_kernel(x_ref, xh_ref, xm_ref, g_ref, w_ref, wdt_ref, cw_ref, cb_ref,
                    o_ref, odt_ref, u_ref, uh_ref, slab_ref, *, n_plain, tiles_per_seq):
    i = pl.program_id(0)
    j = pl.program_id(1)
    tm = x_ref.shape[0]
    tn = w_ref.shape[1]

    @pl.when(j == 0)
    def _():
        _store_normed(x_ref, g_ref, u_ref)
        odt_ref[...] = _dot(u_ref[...], wdt_ref[...])
        if tiles_per_seq is not None:
            @pl.when(i % tiles_per_seq == 0)
            def _():
                _store_normed(xm_ref, g_ref, uh_ref)

            @pl.when(i % tiles_per_seq != 0)
            def _():
                _store_normed(xh_ref, g_ref, uh_ref)

    @pl.when(j < n_plain)
    def _():
        o_ref[...] = _dot(u_ref[...], w_ref[...])

    @pl.when(j >= n_plain)
    def _():
        raw = _dot(u_ref[...], w_ref[...])
        if tiles_per_seq is not None:
            raw_halo = _dot(uh_ref[...], w_ref[...])
        else:
            raw_halo = jnp.zeros((HALO, tn), F32)
        slab = min(tm, 256)
        for t in range(tn // LANES):
            cols = slice(t * LANES, (t + 1) * LANES)
            slab_ref[t, 0:HALO, :] = raw_halo[:, cols]
            slab_ref[t, HALO:HALO + tm, :] = raw[:, cols]
            for r in range(0, tm, slab):
                acc = cb_ref[:, cols]
                for k in range(SSD_CONV):
                    s = HALO - (SSD_CONV - 1) + k + r
                    acc = acc + cw_ref[k:k + 1, cols] * slab_ref[t, s:s + slab, :]
                o_ref[r:r + slab, cols] = _silu(acc)


def _in_proj(x2d, x_meta, gain, w_zx, w_dt, conv_w, conv_b, tm, seq):
    m, k = x2d.shape
    n = w_zx.shape[1]
    tn = 512
    n_plain = (n - conv_w.shape[1]) // tn
    kern = functools.partial(_in_proj_kernel, n_plain=n_plain,
                             tiles_per_seq=None if seq is None else seq // tm)
    conv_col = lambda i, j: (0, jnp.maximum(j - n_plain, 0))
    return pl.pallas_call(
        kern,
        grid=(m // tm, n // tn),
        in_specs=[
            pl.BlockSpec((tm, k), lambda i, j: (i, 0)),
            pl.BlockSpec((HALO, k), lambda i, j: (jnp.maximum(i * (tm // HALO) - 1, 0), 0)),
            pl.BlockSpec((HALO, k), lambda i, j: (0, 0)),
            pl.BlockSpec((1, k), lambda i, j: (0, 0)),
            pl.BlockSpec((k, tn), lambda i, j: (0, j)),
            pl.BlockSpec((k, LANES), lambda i, j: (0, 0)),
            pl.BlockSpec((SSD_CONV, tn), conv_col),
            pl.BlockSpec((1, tn), conv_col),
        ],
        out_specs=[
            pl.BlockSpec((tm, tn), lambda i, j: (i, j)),
            pl.BlockSpec((tm, LANES), lambda i, j: (i, 0)),
        ],
        out_shape=[jax.ShapeDtypeStruct((m, n), F32), jax.ShapeDtypeStruct((m, LANES), F32)],
        scratch_shapes=[pltpu.VMEM((tm, k), BF16), pltpu.VMEM((HALO, k), BF16),
                        pltpu.VMEM((tn // LANES, HALO + tm, LANES), F32)],
        compiler_params=_params(("parallel", "arbitrary"), 48),
        name="in_proj",
    )(x2d, x2d, x_meta, gain, w_zx, w_dt, conv_w, conv_b)


def _ssd_kernel(z0_ref, z1_ref, xa_ref, xb_ref, bc_ref, dt_ref, dtb_ref,
                alog_ref, dsk_ref, ng_ref, st0_ref, sel_ref,
                g_ref, stout_ref, state_ref, *, first_valid_row):
    c = pl.program_id(1)
    q_len = CHUNK
    d_inner = g_ref.shape[1]
    gw = d_inner // SSD_GROUPS
    half = xa_ref.shape[1]
    n_bc = SSD_GROUPS * SSD_STATE

    @pl.when(c == 0)
    def _():
        state_ref[...] = st0_ref[...]

    row = lax.broadcasted_iota(jnp.int32, (q_len, 1), 0)
    valid = row >= first_valid_row

    nh = dtb_ref.shape[1]
    dtr = dt_ref[:, 0:nh] + dtb_ref[...]
    dt = jnp.maximum(dtr, 0.0) + jnp.log1p(jnp.exp(-jnp.abs(dtr)))
    if first_valid_row:
        dt = jnp.where(valid, dt, 0.0)
    dta = dt * (-jnp.exp(alog_ref[...]))

    ri = lax.broadcasted_iota(jnp.int32, (q_len, q_len), 0)
    ci = lax.broadcasted_iota(jnp.int32, (q_len, q_len), 1)
    tri_incl = (ri >= ci).astype(F32)
    eye = (ri == ci).astype(F32)
    cum = jnp.dot(tri_incl, dta, precision=HIGHEST, preferred_element_type=F32)
    cum_t = _dot_tn(dta, (ri <= ci).astype(F32), precision=HIGHEST)
    dt_t = _dot_tn(dt, eye, precision=HIGHEST)
    w_in = jnp.exp(cum[q_len - 1:q_len, :] - cum) * dt

    lane = lax.broadcasted_iota(jnp.int32, (q_len, LANES), 1)
    rowq = lax.broadcasted_iota(jnp.int32, (q_len, LANES), 0)
    left = lane < SSD_HEADDIM
    causal2 = rowq >= jnp.where(left, lane, lane - SSD_HEADDIM)
    rr = lax.broadcasted_iota(jnp.int32, (2 * q_len, LANES), 0)
    ll = lax.broadcasted_iota(jnp.int32, (2 * q_len, LANES), 1)
    blockdiag = (rr < q_len) == (ll < SSD_HEADDIM)

    cum_hi = cum.astype(BF16)
    cum_r = cum - cum_hi.astype(F32)
    cum_mid = cum_r.astype(BF16)
    cum_lo = (cum_r - cum_mid.astype(F32)).astype(BF16)
    w16 = w_in.astype(BF16)
    zero16 = jnp.zeros_like(w16)
    spread_lhs = jnp.concatenate(
        [jnp.concatenate([cum_hi, cum_mid, cum_lo], axis=1),
         jnp.concatenate([w16, zero16, zero16], axis=1)], axis=0)

    def pair_rows(a, h):
        return jnp.concatenate([a[h:h + 1, :], a[h + 1:h + 2, :]], axis=1)

    heads_per_group = gw // SSD_HEADDIM
    pairs = heads_per_group // 2

    def x_pair(g, p):
        xr = xa_ref if (g * gw) < half else xb_ref
        lo = (g * gw) % half + p * LANES
        return xr[:, lo:lo + LANES]

    def run_groups(groups):
        bms, cb2s, y_offs, spreads = {}, {}, {}, {}
        for g in groups:
            bm = bc_ref[:, g * SSD_STATE:(g + 1) * SSD_STATE]
            cm16 = bc_ref[:, n_bc + g * SSD_STATE:n_bc + (g + 1) * SSD_STATE].astype(BF16)
            bms[g] = bm
            cb2s[g] = _dot_nt(cm16, jnp.concatenate([bm, bm], axis=0).astype(BF16))
            y_offs[g] = _dot(cm16, state_ref[g].astype(BF16))
            spreads[g] = _dot(spread_lhs, sel_ref[:, g * gw:(g + 1) * gw])

        m2s, xbds, ea2s = {}, {}, {}
        for g in groups:
            for p in range(pairs):
                h = g * heads_per_group + 2 * p
                a2 = spreads[g][0:q_len, p * LANES:(p + 1) * LANES]
                seg = a2 - pair_rows(cum_t, h)
                m2 = cb2s[g] * jnp.exp(jnp.where(causal2, seg, -jnp.inf)) * pair_rows(dt_t, h)
                xs2 = x_pair(g, p)
                m2s[g, p] = m2.astype(BF16)
                xbds[g, p] = jnp.where(blockdiag, jnp.concatenate([xs2, xs2], axis=0), 0.0).astype(BF16)
                ea2s[g, p] = jnp.exp(a2)

        y_diags = {gp: _dot(m2s[gp], xbds[gp]) for gp in m2s}

        for g in groups:
            ys, xws, decs = [], [], []
            for p in range(pairs):
                sl = slice(p * LANES, (p + 1) * LANES)
                xs2 = x_pair(g, p)
                ys.append(y_diags[g, p] + y_offs[g][:, sl] * ea2s[g, p]
                          + xs2 * dsk_ref[:, g * gw + p * LANES:g * gw + (p + 1) * LANES])
                xws.append((xs2 * spreads[g][q_len:2 * q_len, sl]).astype(BF16))
                decs.append(ea2s[g, p][q_len - 1:q_len, :])
            xw = jnp.concatenate(xws, axis=1)
            dec = jnp.concatenate(decs, axis=1)
            state_ref[g] = state_ref[g] * dec + _dot(bms[g].T.astype(BF16), xw)
            y = jnp.concatenate(ys, axis=1)
            zr = z0_ref if (g * gw) < half else z1_ref
            zc = (g * gw) % half
            gt = y * _silu(zr[:, zc:zc + gw])
            ms = jnp.mean(gt * gt, axis=-1, keepdims=True)
            g_ref[:, g * gw:(g + 1) * gw] = (gt * lax.rsqrt(ms + EPS) * ng_ref[:, g * gw:(g + 1) * gw]).astype(g_ref.dtype)

    for g0 in range(0, SSD_GROUPS, SSD_GROUP_BATCH):
        run_groups(range(g0, g0 + SSD_GROUP_BATCH))

    @pl.when(c == pl.num_programs(1) - 1)
    def _():
        stout_ref[...] = state_ref[...]


def _ssd(zx, dtraw, dt_bias, a_log, d_exp, norm_g, st0, *, nb, nck, first_valid_row):
    d_inner = norm_g.shape[1]
    half = 2 * SSD_GROUPS * SSD_STATE
    assert d_inner == 2 * half and zx.shape[1] == 2 * d_inner + half
    gw = d_inner // SSD_GROUPS
    nh = dt_bias.shape[1]
    sel = ((jnp.arange(3 * nh)[:, None] % nh) == (jnp.arange(d_inner)[None, :] // SSD_HEADDIM)).astype(BF16)

    def rows(col):
        return pl.BlockSpec((CHUNK, half), lambda b, c: (b * nck + c, col))

    def full(a):
        nd = a.ndim
        return pl.BlockSpec(a.shape, lambda b, c: (0,) * nd)

    kern = functools.partial(_ssd_kernel, first_valid_row=first_valid_row)
    return pl.pallas_call(
        kern,
        grid=(nb, nck),
        in_specs=[rows(0), rows(1), rows(2), rows(3), rows(4),
                  pl.BlockSpec((CHUNK, LANES), lambda b, c: (b * nck + c, 0)),
                  full(dt_bias), full(a_log), full(d_exp), full(norm_g), full(st0), full(sel)],
        out_specs=[pl.BlockSpec((CHUNK, d_inner), lambda b, c: (b * nck + c, 0)),
                   pl.BlockSpec((None, SSD_GROUPS, SSD_STATE, gw), lambda b, c: (b, 0, 0, 0))],
        out_shape=[jax.ShapeDtypeStruct((nb * nck * CHUNK, d_inner), BF16),
                   jax.ShapeDtypeStruct((nb, SSD_GROUPS, SSD_STATE, gw), F32)],
        scratch_shapes=[pltpu.VMEM((SSD_GROUPS, SSD_STATE, gw), F32)],
        compiler_params=_params(("parallel", "arbitrary"), 32),
        name="ssd_scan",
    )(zx, zx, zx, zx, zx, dtraw, dt_bias, a_log, d_exp, norm_g, st0, sel)


def _mm_res_kernel(a_ref, w_ref, h_ref, o_ref):
    o_ref[...] = h_ref[...] + _dot(a_ref[...], w_ref[...])


def _mm_res(a, w, h, tm, name):
    m, k = a.shape
    n = w.shape[1]
    return pl.pallas_call(
        _mm_res_kernel,
        grid=(m // tm,),
        in_specs=[pl.BlockSpec((tm, k), lambda i: (i, 0)),
                  pl.BlockSpec((k, n), lambda i: (0, 0), pipeline_mode=pl.Buffered(1)),
                  pl.BlockSpec((tm, n), lambda i: (i, 0))],
        out_specs=pl.BlockSpec((tm, n), lambda i: (i, 0)),
        out_shape=jax.ShapeDtypeStruct((m, n), F32),
        compiler_params=_params(("parallel",), 48),
        name=name,
    )(a, w, h)


def _ffn_kernel(h_ref, g_ref, wg_ref, wu_ref, wd_ref, fg_ref, o_ref, u_ref, *, final_norm):
    j = pl.program_id(1)

    @pl.when(j == 0)
    def _():
        _store_normed(h_ref, g_ref, u_ref)
        o_ref[...] = h_ref[...]

    u = u_ref[...]
    gate = _dot(u, wg_ref[...])
    up = _dot(u, wu_ref[...])
    act = (_silu(gate) * up).astype(BF16)
    o_ref[...] += _dot(act, wd_ref[...])

    if final_norm:
        @pl.when(j == pl.num_programs(1) - 1)
        def _():
            rows = o_ref.shape[0]
            slab = min(rows, 256)
            for r in range(0, rows, slab):
                x = o_ref[r:r + slab, :]
                ms = jnp.mean(x * x, axis=-1, keepdims=True)
                o_ref[r:r + slab, :] = x * lax.rsqrt(ms + EPS) * fg_ref[...]


def _ffn(h, gain, w_gate_up, w_down, final_gain, tm, final_norm, name):
    m, d = h.shape
    f = w_down.shape[0]
    tf = 512
    nf = f // tf
    kern = functools.partial(_ffn_kernel, final_norm=final_norm)
    return pl.pallas_call(
        kern,
        grid=(m // tm, nf),
        in_specs=[pl.BlockSpec((tm, d), lambda i, j: (i, 0)),
                  pl.BlockSpec((1, d), lambda i, j: (0, 0)),
                  pl.BlockSpec((d, tf), lambda i, j: (0, j)),
                  pl.BlockSpec((d, tf), lambda i, j: (0, nf + j)),
                  pl.BlockSpec((tf, d), lambda i, j: (j, 0)),
                  pl.BlockSpec((1, d), lambda i, j: (0, 0))],
        out_specs=pl.BlockSpec((tm, d), lambda i, j: (i, 0)),
        out_shape=jax.ShapeDtypeStruct((m, d), F32),
        scratch_shapes=[pltpu.VMEM((tm, d), BF16)],
        compiler_params=_params(("parallel", "arbitrary"), 48),
        name=name,
    )(h, gain, w_gate_up, w_gate_up, w_down, final_gain)


def _qkv_kernel(h_ref, gq_ref, gkv_ref, w_ref, o_ref, uq_ref, ukv_ref, *, nq_tiles):
    j = pl.program_id(1)

    @pl.when(j == 0)
    def _():
        rows = h_ref.shape[0]
        slab = min(rows, 256)
        for r in range(0, rows, slab):
            x = h_ref[r:r + slab, :]
            xn = x * lax.rsqrt(jnp.mean(x * x, axis=-1, keepdims=True) + EPS)
            uq_ref[r:r + slab, :] = (xn * gq_ref[...]).astype(BF16)
            ukv_ref[r:r + slab, :] = (xn * gkv_ref[...]).astype(BF16)

    @pl.when(j < nq_tiles)
    def _():
        o_ref[...] = _dot(uq_ref[...], w_ref[...]).astype(o_ref.dtype)

    @pl.when(j >= nq_tiles)
    def _():
        o_ref[...] = _dot(ukv_ref[...], w_ref[...]).astype(o_ref.dtype)


def _qkv(h, gq, gkv, w_qkv, nq_cols, tm):
    m, d = h.shape
    n = w_qkv.shape[1]
    tn = 512
    kern = functools.partial(_qkv_kernel, nq_tiles=nq_cols // tn)
    return pl.pallas_call(
        kern,
        grid=(m // tm, n // tn),
        in_specs=[pl.BlockSpec((tm, d), lambda i, j: (i, 0)),
                  pl.BlockSpec((1, d), lambda i, j: (0, 0)),
                  pl.BlockSpec((1, d), lambda i, j: (0, 0)),
                  pl.BlockSpec((d, tn), lambda i, j: (0, j))],
        out_specs=pl.BlockSpec((tm, tn), lambda i, j: (i, j)),
        out_shape=jax.ShapeDtypeStruct((m, n), BF16),
        scratch_shapes=[pltpu.VMEM((tm, d), BF16), pltpu.VMEM((tm, d), BF16)],
        compiler_params=_params(("parallel", "arbitrary"), 48),
        name="qkv_proj",
    )(h, gq, gkv, w_qkv)


def _attn_kernel(q_ref, k_ref, v_ref, km_ref, vm_ref, o_ref, acc_ref, c_ref, *, scale):
    t = q_ref.shape[0]
    hd = SB_HEAD_DIM
    n_par = q_ref.shape[1] // hd
    qi = pl.program_id(2)
    ri = lax.broadcasted_iota(jnp.int32, (t, t), 0)
    ci = lax.broadcasted_iota(jnp.int32, (t, t), 1)
    upper = (ri > ci).astype(BF16)
    upper2 = jnp.concatenate([upper, upper], axis=0)

    def step(k_of, v_of, mask, u2):
        heads = [slice(g * hd, (g + 1) * hd) for g in range(n_par)]
        zs = [_dot_nt(q_ref[:, cols], k_of(cols)) * (scale * LOG2E) for cols in heads]
        ls2s, lk2s, hilos = [], [], []
        for z2 in zs:
            ls2 = jnp.minimum(z2, 0.0) - jnp.log(1.0 + jnp.exp2(-jnp.abs(z2))) * LOG2E
            lk2 = ls2 - z2
            if mask is not None:
                lk2 = jnp.where(mask, lk2, 0.0)
            hi = lk2.astype(BF16)
            lo = (lk2 - hi.astype(F32)).astype(BF16)
            ls2s.append(ls2)
            lk2s.append(lk2)
            hilos.append(jnp.concatenate([hi, lo], axis=1))
        tails = [_dot(hl, u2) for hl in hilos]
        cmax = None
        ps = []
        for g in range(n_par):
            c = c_ref[g]
            p = jnp.exp2(ls2s[g] + tails[g] + c)
            if mask is not None:
                p = jnp.where(mask, p, 0.0)
            ps.append(p.astype(BF16))
            c_new = c + jnp.sum(lk2s[g], axis=1, keepdims=True)
            c_ref[g] = c_new
            cmax = c_new if cmax is None else jnp.maximum(cmax, c_new)
        for g in range(n_par):
            acc_ref[g] += _dot(ps[g], v_of(heads[g]))
        return (jnp.max(cmax) > EXP_ZERO_F32 * LOG2E).astype(jnp.int32)

    acc_ref[...] = jnp.zeros_like(acc_ref)
    c_ref[...] = jnp.zeros_like(c_ref)

    def block(kb):
        s = pl.multiple_of(kb * t, t)
        return (lambda cols: k_ref[pl.ds(s, t), cols]), (lambda cols: v_ref[pl.ds(s, t), cols])

    go = step(*block(qi), ci < ri, upper2)

    def cond(carry):
        kb, alive = carry
        return jnp.logical_and(kb >= 0, alive > 0)

    def body(carry):
        kb, _ = carry
        return kb - 1, step(*block(kb), None, upper2)

    _, go = lax.while_loop(cond, body, (qi - 1, go))

    @pl.when(go > 0)
    def _():
        tm = km_ref.shape[0]
        mi = lax.broadcasted_iota(jnp.int32, (t, tm), 1)
        mr = lax.broadcasted_iota(jnp.int32, (tm, tm), 0)
        mc = lax.broadcasted_iota(jnp.int32, (tm, tm), 1)
        um = (mr > mc).astype(BF16)
        step(lambda cols: km_ref[:, cols], lambda cols: vm_ref[:, cols], mi < N_META,
             jnp.concatenate([um, um], axis=0))

    for g in range(n_par):
        o_ref[:, g * hd:(g + 1) * hd] = acc_ref[g].astype(o_ref.dtype)


def _attention(qkv, kv_meta, nb, seq, n_heads):
    t = ATTN_BLOCK
    n_par = ATTN_HEADS_PER_STEP
    w = n_par * SB_HEAD_DIM
    ng = n_heads // n_par
    kern = functools.partial(_attn_kernel, scale=SB_HEAD_DIM ** -0.5)
    qkv3 = qkv.reshape(nb, seq, qkv.shape[1])
    return pl.pallas_call(
        kern,
        grid=(nb, ng, seq // t),
        in_specs=[pl.BlockSpec((None, t, w), lambda b, h, i: (b, i, h)),
                  pl.BlockSpec((None, seq, w), lambda b, h, i: (b, 0, ng + h)),
                  pl.BlockSpec((None, seq, w), lambda b, h, i: (b, 0, 2 * ng + h)),
                  pl.BlockSpec((LANES, w), lambda b, h, i: (0, ng + h)),
                  pl.BlockSpec((LANES, w), lambda b, h, i: (0, 2 * ng + h))],
        out_specs=pl.BlockSpec((None, t, w), lambda b, h, i: (b, i, h)),
        out_shape=jax.ShapeDtypeStruct((nb, seq, n_heads * SB_HEAD_DIM), BF16),
        scratch_shapes=[pltpu.VMEM((n_par, t, SB_HEAD_DIM), F32), pltpu.VMEM((n_par, t, 1), F32)],
        compiler_params=_params(("parallel", "parallel", "arbitrary"), 48),
        name="sb_attention",
    )(qkv3, qkv3, qkv3, kv_meta, kv_meta)


def _row_tile(m):
    return 1024 if m % 1024 == 0 else m


def kernel(x, meta_tokens, norm_mix, norm_ffn, ssd_in_proj, ssd_conv_w, ssd_conv_b, ssd_dt_bias,
           ssd_a_log, ssd_d, ssd_norm, ssd_out_proj, kv_norm, w_kv, sb_w_q, sb_w_o, ffn_gate_up,
           ffn_down, final_norm):
    nb, seq, d = x.shape
    d_inner = ssd_out_proj.shape[1]
    conv_dim = ssd_conv_w.shape[2]
    n_ssd_heads = ssd_dt_bias.shape[1]
    sb_width = sb_w_q.shape[2]
    n_sb_heads = sb_width // SB_HEAD_DIM
    assert seq % ATTN_BLOCK == 0 and seq % CHUNK == 0 and n_ssd_heads <= LANES
    assert meta_tokens.shape[0] == N_META == HALO

    w_in = ssd_in_proj[0]
    w_zx = w_in[:, :d_inner + conv_dim].astype(BF16)
    w_dt = jnp.pad(w_in[:, d_inner + conv_dim:], ((0, 0), (0, LANES - n_ssd_heads))).astype(BF16)
    w_out = ssd_out_proj[0].astype(BF16)
    w_qkv = jnp.concatenate([sb_w_q[0].astype(BF16), w_kv.astype(BF16)], axis=1)
    w_o = sb_w_o[0].astype(BF16)
    w_gu = [ffn_gate_up[l].astype(BF16) for l in range(2)]
    w_dn = [ffn_down[l].astype(BF16) for l in range(2)]
    row = lambda v: v.reshape(1, -1).astype(F32)
    d_exp = row(jnp.repeat(ssd_d[0], SSD_HEADDIM))
    conv_w = ssd_conv_w[0].astype(F32)
    conv_b = row(ssd_conv_b[0])
    ssd_consts = (row(ssd_dt_bias[0]), row(ssd_a_log[0]), d_exp, row(ssd_norm[0]))
    gw = d_inner // SSD_GROUPS

    hm = meta_tokens.astype(F32)
    zx_m, dt_m = _in_proj(hm, hm, row(norm_mix[0]), w_zx, w_dt, conv_w, conv_b, N_META, None)
    lead = CHUNK - N_META
    g_m, st_m = _ssd(jnp.pad(zx_m, ((lead, 0), (0, 0))), jnp.pad(dt_m, ((lead, 0), (0, 0))),
                     *ssd_consts, jnp.zeros((SSD_GROUPS, SSD_STATE, gw), F32),
                     nb=1, nck=1, first_valid_row=lead)
    hm = _mm_res(g_m[lead:], w_out, hm, N_META, "out_proj_meta")
    hm = _ffn(hm, row(norm_ffn[0]), w_gu[0], w_dn[0], row(final_norm), N_META, False, "ffn0_meta")
    qkv_m = _qkv(hm, row(norm_mix[1]), row(kv_norm), w_qkv, sb_width, N_META)
    kv_meta = jnp.pad(qkv_m, ((0, LANES - N_META), (0, 0)))

    tm = _row_tile(nb * seq)
    h = x.reshape(nb * seq, d).astype(F32)
    zx, dtr = _in_proj(h, meta_tokens.astype(F32), row(norm_mix[0]), w_zx, w_dt, conv_w, conv_b, tm, seq)
    g, _ = _ssd(zx, dtr, *ssd_consts, st_m[0], nb=nb, nck=seq // CHUNK, first_valid_row=0)
    h = _mm_res(g, w_out, h, tm // 2, "out_proj")
    h = _ffn(h, row(norm_ffn[0]), w_gu[0], w_dn[0], row(final_norm), tm // 2, False, "ffn0")

    qkv = _qkv(h, row(norm_mix[1]), row(kv_norm), w_qkv, sb_width, tm)
    o = _attention(qkv, kv_meta, nb, seq, n_sb_heads)
    h = _mm_res(o.reshape(nb * seq, sb_width), w_o, h, tm // 2, "o_proj")
    out = _ffn(h, row(norm_ffn[1]), w_gu[1], w_dn[1], row(final_norm), tm // 2, True, "ffn1")
    return out.reshape(nb, seq, d)
```

```python
import functools

import jax
import jax.numpy as jnp
from jax import lax
from jax.experimental import pallas as pl
from jax.experimental.pallas import tpu as pltpu

F32 = jnp.float32
BF16 = jnp.bfloat16
EPS = 1e-6
CHUNK = 64
N_META = 16
HALO = 16
SSD_HEADDIM = 64
SSD_GROUPS = 8
SSD_STATE = 128
SSD_CONV = 4
SSD_GROUP_BATCH = 4
FFN_HIDDEN_TILE = 512
SB_HEAD_DIM = 128
LANES = 128
ATTN_BLOCK = 256
ATTN_HEADS_PER_STEP = 4
LOG2E = 1.4426950408889634
EXP_ZERO_F32 = -104.0
MIB = 1024 * 1024
HIGHEST = lax.Precision.HIGHEST


def _params(sem, vmem_mib):
    return pltpu.CompilerParams(dimension_semantics=sem, vmem_limit_bytes=vmem_mib * MIB)


def _dot(a, b):
    return jnp.dot(a, b, preferred_element_type=F32)


def _dot_nt(a, b):
    return lax.dot_general(a, b, (((1,), (1,)), ((), ())), preferred_element_type=F32)


def _dot_tn(a, b, precision=None):
    return lax.dot_general(a, b, (((0,), (0,)), ((), ())), precision=precision,
                           preferred_element_type=F32)


def _silu(x):
    hx = 0.5 * x
    return hx + hx * jnp.tanh(hx)


def _store_normed(x_ref, g_ref, u_ref):
    rows = x_ref.shape[0]
    slab = min(rows, 256)
    for r in range(0, rows, slab):
        x = x_ref[r:r + slab, :]
        ms = jnp.mean(x * x, axis=-1, keepdims=True)
        u_ref[r:r + slab, :] = (x * lax.rsqrt(ms + EPS) * g_ref[...]).astype(BF16)


def _in_proj_kernel(x_ref, xh_ref, xm_ref, g_ref, w_ref, wdt_ref, cw_ref, cb_ref,
                    o_ref, odt_ref, u_ref, uh_ref, slab_ref, *, n_plain, tiles_per_seq):
    i = pl.program_id(0)
    j = pl.program_id(1)
    tm = x_ref.shape[0]
    tn = w_ref.shape[1]
    slab_rows = min(tm, 256)

    @pl.when(j == 0)
    def _():
        _store_normed(x_ref, g_ref, u_ref)
        odt_ref[...] = _dot(u_ref[...], wdt_ref[...])
        if tiles_per_seq is not None:
            @pl.when(i % tiles_per_seq == 0)
            def _():
                _store_normed(xm_ref, g_ref, uh_ref)

            @pl.when(i % tiles_per_seq != 0)
            def _():
                _store_normed(xh_ref, g_ref, uh_ref)

    @pl.when(j < n_plain)
    def _():
        o_ref[...] = _dot(u_ref[...], w_ref[...]).astype(o_ref.dtype)

    @pl.when(j >= n_plain)
    def _():
        raw = _dot(u_ref[...], w_ref[...])
        if tiles_per_seq is not None:
            raw_halo = _dot(uh_ref[...], w_ref[...])
        else:
            raw_halo = jnp.zeros((HALO, tn), F32)
        for t in range(tn // LANES):
            cols = slice(t * LANES, (t + 1) * LANES)
            slab_ref[t, 0:HALO, :] = raw_halo[:, cols]
            slab_ref[t, HALO:HALO + tm, :] = raw[:, cols]
            for r in range(0, tm, slab_rows):
                acc = cb_ref[:, cols]
                for k in range(SSD_CONV):
                    s = HALO - (SSD_CONV - 1) + k + r
                    acc = acc + cw_ref[k:k + 1, cols] * slab_ref[t, s:s + slab_rows, :]
                o_ref[r:r + slab_rows, cols] = _silu(acc).astype(o_ref.dtype)


def _in_proj(x2d, x_meta, gain, w_in, w_dt, conv_w, conv_b, tm, seq):
    m, k = x2d.shape
    tn = 512
    n = (w_in.shape[2] // tn) * tn
    n_plain = (n - conv_w.shape[1]) // tn
    assert n == n_plain * tn + conv_w.shape[1]
    kern = functools.partial(_in_proj_kernel, n_plain=n_plain,
                             tiles_per_seq=None if seq is None else seq // tm)
    conv_col = lambda i, j: (0, jnp.maximum(j - n_plain, 0))
    return pl.pallas_call(
        kern,
        grid=(m // tm, n // tn),
        in_specs=[
            pl.BlockSpec((tm, k), lambda i, j: (i, 0)),
            pl.BlockSpec((HALO, k), lambda i, j: (jnp.maximum(i * (tm // HALO) - 1, 0), 0)),
            pl.BlockSpec((HALO, k), lambda i, j: (0, 0)),
            pl.BlockSpec((1, k), lambda i, j: (0, 0)),
            pl.BlockSpec((None, k, tn), lambda i, j: (0, 0, j)),
            pl.BlockSpec((k, LANES), lambda i, j: (0, 0)),
            pl.BlockSpec((SSD_CONV, tn), conv_col),
            pl.BlockSpec((1, tn), conv_col),
        ],
        out_specs=[
            pl.BlockSpec((tm, tn), lambda i, j: (i, j)),
            pl.BlockSpec((tm, LANES), lambda i, j: (i, 0)),
        ],
        out_shape=[jax.ShapeDtypeStruct((m, n), BF16), jax.ShapeDtypeStruct((m, LANES), F32)],
        scratch_shapes=[pltpu.VMEM((tm, k), BF16), pltpu.VMEM((HALO, k), BF16),
                        pltpu.VMEM((tn // LANES, HALO + tm, LANES), F32)],
        compiler_params=_params(("parallel", "arbitrary"), 48),
        name="in_proj",
    )(x2d, x2d, x_meta, gain, w_in, w_dt, conv_w, conv_b)


def _ssd_kernel(z0_ref, z1_ref, xa_ref, xb_ref, bc_ref, dt_ref, dtb_ref,
                alog_ref, dsk_ref, ng_ref, st0_ref, sel_ref,
                g_ref, stout_ref, state_ref, *, first_valid_row):
    c = pl.program_id(1)
    q_len = CHUNK
    d_inner = g_ref.shape[1]
    gw = d_inner // SSD_GROUPS
    half = xa_ref.shape[1]
    n_bc = SSD_GROUPS * SSD_STATE

    @pl.when(c == 0)
    def _():
        state_ref[...] = st0_ref[...]

    row = lax.broadcasted_iota(jnp.int32, (q_len, 1), 0)
    valid = row >= first_valid_row

    nh = dtb_ref.shape[1]
    dtr = dt_ref[:, 0:nh] + dtb_ref[...]
    dt = jnp.maximum(dtr, 0.0) + jnp.log1p(jnp.exp(-jnp.abs(dtr)))
    if first_valid_row:
        dt = jnp.where(valid, dt, 0.0)
    dta = dt * (-jnp.exp(alog_ref[...]))

    ri = lax.broadcasted_iota(jnp.int32, (q_len, q_len), 0)
    ci = lax.broadcasted_iota(jnp.int32, (q_len, q_len), 1)
    tri_incl = (ri >= ci).astype(F32)
    eye = (ri == ci).astype(F32)
    cum = jnp.dot(tri_incl, dta, precision=HIGHEST, preferred_element_type=F32)
    cum_t = _dot_tn(dta, (ri <= ci).astype(F32), precision=HIGHEST)
    dt_t = _dot_tn(dt, eye, precision=HIGHEST)
    w_in = jnp.exp(cum[q_len - 1:q_len, :] - cum) * dt

    lane = lax.broadcasted_iota(jnp.int32, (q_len, LANES), 1)
    rowq = lax.broadcasted_iota(jnp.int32, (q_len, LANES), 0)
    left = lane < SSD_HEADDIM
    causal2 = rowq >= jnp.where(left, lane, lane - SSD_HEADDIM)
    rr = lax.broadcasted_iota(jnp.int32, (2 * q_len, LANES), 0)
    ll = lax.broadcasted_iota(jnp.int32, (2 * q_len, LANES), 1)
    blockdiag = (rr < q_len) == (ll < SSD_HEADDIM)

    cum_hi = cum.astype(BF16)
    cum_r = cum - cum_hi.astype(F32)
    cum_mid = cum_r.astype(BF16)
    cum_lo = (cum_r - cum_mid.astype(F32)).astype(BF16)
    w16 = w_in.astype(BF16)
    zero16 = jnp.zeros_like(w16)
    spread_lhs = jnp.concatenate(
        [jnp.concatenate([cum_hi, cum_mid, cum_lo], axis=1),
         jnp.concatenate([w16, zero16, zero16], axis=1)], axis=0)

    def pair_rows(a, h):
        return jnp.concatenate([a[h:h + 1, :], a[h + 1:h + 2, :]], axis=1)

    heads_per_group = gw // SSD_HEADDIM
    pairs = heads_per_group // 2

    def x_pair(g, p):
        xr = xa_ref if (g * gw) < half else xb_ref
        lo = (g * gw) % half + p * LANES
        return xr[:, lo:lo + LANES].astype(F32)

    def run_groups(groups):
        bms, cb2s, y_offs, spreads = {}, {}, {}, {}
        for g in groups:
            bm = bc_ref[:, g * SSD_STATE:(g + 1) * SSD_STATE].astype(F32)
            cm16 = bc_ref[:, n_bc + g * SSD_STATE:n_bc + (g + 1) * SSD_STATE].astype(BF16)
            bms[g] = bm
            cb2s[g] = _dot_nt(cm16, jnp.concatenate([bm, bm], axis=0).astype(BF16))
            y_offs[g] = _dot(cm16, state_ref[g].astype(BF16))
            spreads[g] = _dot(spread_lhs, sel_ref[:, g * gw:(g + 1) * gw])

        m2s, xbds, ea2s = {}, {}, {}
        for g in groups:
            for p in range(pairs):
                h = g * heads_per_group + 2 * p
                a2 = spreads[g][0:q_len, p * LANES:(p + 1) * LANES]
                seg = a2 - pair_rows(cum_t, h)
                m2 = cb2s[g] * jnp.exp(jnp.where(causal2, seg, -jnp.inf)) * pair_rows(dt_t, h)
                xs2 = x_pair(g, p)
                m2s[g, p] = m2.astype(BF16)
                xbds[g, p] = jnp.where(blockdiag, jnp.concatenate([xs2, xs2], axis=0), 0.0).astype(BF16)
                ea2s[g, p] = jnp.exp(a2)

        y_diags = {gp: _dot(m2s[gp], xbds[gp]) for gp in m2s}

        for g in groups:
            ys, xws, decs = [], [], []
            for p in range(pairs):
                sl = slice(p * LANES, (p + 1) * LANES)
                xs2 = x_pair(g, p)
                ys.append(y_diags[g, p] + y_offs[g][:, sl] * ea2s[g, p]
                          + xs2 * dsk_ref[:, g * gw + p * LANES:g * gw + (p + 1) * LANES])
                xws.append((xs2 * spreads[g][q_len:2 * q_len, sl]).astype(BF16))
                decs.append(ea2s[g, p][q_len - 1:q_len, :])
            xw = jnp.concatenate(xws, axis=1)
            dec = jnp.concatenate(decs, axis=1)
            state_ref[g] = state_ref[g] * dec + _dot(bms[g].T.astype(BF16), xw)
            y = jnp.concatenate(ys, axis=1)
            zr = z0_ref if (g * gw) < half else z1_ref
            zc = (g * gw) % half
            gt = y * _silu(zr[:, zc:zc + gw].astype(F32))
            ms = jnp.mean(gt * gt, axis=-1, keepdims=True)
            g_ref[:, g * gw:(g + 1) * gw] = (gt * lax.rsqrt(ms + EPS) * ng_ref[:, g * gw:(g + 1) * gw]).astype(g_ref.dtype)

    for g0 in range(0, SSD_GROUPS, SSD_GROUP_BATCH):
        run_groups(range(g0, g0 + SSD_GROUP_BATCH))

    @pl.when(c == pl.num_programs(1) - 1)
    def _():
        stout_ref[...] = state_ref[...]


def _ssd(zx, dtraw, dt_bias, a_log, d_exp, norm_g, st0, *, nb, nck, first_valid_row):
    d_inner = norm_g.shape[1]
    half = 2 * SSD_GROUPS * SSD_STATE
    assert d_inner == 2 * half and zx.shape[1] == 2 * d_inner + half
    gw = d_inner // SSD_GROUPS
    nh = dt_bias.shape[1]
    sel = ((jnp.arange(3 * nh)[:, None] % nh) == (jnp.arange(d_inner)[None, :] // SSD_HEADDIM)).astype(BF16)

    def rows(col):
        return pl.BlockSpec((CHUNK, half), lambda b, c: (b * nck + c, col))

    def full(a):
        nd = a.ndim
        return pl.BlockSpec(a.shape, lambda b, c: (0,) * nd)

    kern = functools.partial(_ssd_kernel, first_valid_row=first_valid_row)
    return pl.pallas_call(
        kern,
        grid=(nb, nck),
        in_specs=[rows(0), rows(1), rows(2), rows(3), rows(4),
                  pl.BlockSpec((CHUNK, LANES), lambda b, c: (b * nck + c, 0)),
                  full(dt_bias), full(a_log), full(d_exp), full(norm_g), full(st0), full(sel)],
        out_specs=[pl.BlockSpec((CHUNK, d_inner), lambda b, c: (b * nck + c, 0)),
                   pl.BlockSpec((None, SSD_GROUPS, SSD_STATE, gw), lambda b, c: (b, 0, 0, 0))],
        out_shape=[jax.ShapeDtypeStruct((nb * nck * CHUNK, d_inner), BF16),
                   jax.ShapeDtypeStruct((nb, SSD_GROUPS, SSD_STATE, gw), F32)],
        scratch_shapes=[pltpu.VMEM((SSD_GROUPS, SSD_STATE, gw), F32)],
        compiler_params=_params(("parallel", "arbitrary"), 32),
        name="ssd_scan",
    )(zx, zx, zx, zx, zx, dtraw, dt_bias, a_log, d_exp, norm_g, st0, sel)


def _mm_res_kernel(a_ref, w_ref, h_ref, g_ref, o_ref, u_ref):
    o_ref[...] = h_ref[...] + _dot(a_ref[...], w_ref[...])
    _store_normed(o_ref, g_ref, u_ref)


def _mm_res(a, w, h, gain, tm, name):
    m, k = a.shape
    n = w.shape[1]
    return pl.pallas_call(
        _mm_res_kernel,
        grid=(m // tm,),
        in_specs=[pl.BlockSpec((tm, k), lambda i: (i, 0)),
                  pl.BlockSpec((k, n), lambda i: (0, 0), pipeline_mode=pl.Buffered(1)),
                  pl.BlockSpec((tm, n), lambda i: (i, 0)),
                  pl.BlockSpec((1, n), lambda i: (0, 0))],
        out_specs=[pl.BlockSpec((tm, n), lambda i: (i, 0)),
                   pl.BlockSpec((tm, n), lambda i: (i, 0))],
        out_shape=[jax.ShapeDtypeStruct((m, n), F32), jax.ShapeDtypeStruct((m, n), BF16)],
        compiler_params=_params(("parallel",), 56),
        name=name,
    )(a, w, h, gain)


def _ffn_kernel(h_ref, u_ref, wg_ref, wu_ref, wd_ref, fg_ref, o_ref, *, final_norm):
    j = pl.program_id(1)

    @pl.when(j == 0)
    def _():
        o_ref[...] = h_ref[...]

    u = u_ref[...]
    gate = _dot(u, wg_ref[...])
    up = _dot(u, wu_ref[...])
    act = (_silu(gate) * up).astype(BF16)
    o_ref[...] += _dot(act, wd_ref[...])

    if final_norm:
        @pl.when(j == pl.num_programs(1) - 1)
        def _():
            rows = o_ref.shape[0]
            slab = min(rows, 256)
            for r in range(0, rows, slab):
                x = o_ref[r:r + slab, :]
                ms = jnp.mean(x * x, axis=-1, keepdims=True)
                o_ref[r:r + slab, :] = x * lax.rsqrt(ms + EPS) * fg_ref[...]


def _ffn(h, u, w_gate_up, w_down, layer, final_gain, tm, final_norm, name):
    m, d = h.shape
    f = w_down.shape[1]
    tf = FFN_HIDDEN_TILE
    nf = f // tf
    kern = functools.partial(_ffn_kernel, final_norm=final_norm)
    return pl.pallas_call(
        kern,
        grid=(m // tm, nf),
        in_specs=[pl.BlockSpec((tm, d), lambda i, j: (i, 0)),
                  pl.BlockSpec((tm, d), lambda i, j: (i, 0)),
                  pl.BlockSpec((None, d, tf), lambda i, j: (layer, 0, j)),
                  pl.BlockSpec((None, d, tf), lambda i, j: (layer, 0, nf + j)),
                  pl.BlockSpec((None, tf, d), lambda i, j: (layer, j, 0)),
                  pl.BlockSpec((1, d), lambda i, j: (0, 0))],
        out_specs=pl.BlockSpec((tm, d), lambda i, j: (i, 0)),
        out_shape=jax.ShapeDtypeStruct((m, d), F32),
        compiler_params=_params(("parallel", "arbitrary"), 48),
        name=name,
    )(h, u, w_gate_up, w_gate_up, w_down, final_gain)


def _qkv_kernel(h_ref, gq_ref, gkv_ref, w_ref, o_ref, uq_ref, ukv_ref, *, nq_tiles):
    j = pl.program_id(1)

    @pl.when(j == 0)
    def _():
        rows = h_ref.shape[0]
        slab = min(rows, 256)
        for r in range(0, rows, slab):
            x = h_ref[r:r + slab, :]
            xn = x * lax.rsqrt(jnp.mean(x * x, axis=-1, keepdims=True) + EPS)
            uq_ref[r:r + slab, :] = (xn * gq_ref[...]).astype(BF16)
            ukv_ref[r:r + slab, :] = (xn * gkv_ref[...]).astype(BF16)

    @pl.when(j < nq_tiles)
    def _():
        o_ref[...] = _dot(uq_ref[...], w_ref[...]).astype(o_ref.dtype)

    @pl.when(j >= nq_tiles)
    def _():
        o_ref[...] = _dot(ukv_ref[...], w_ref[...]).astype(o_ref.dtype)


def _qkv(h, gq, gkv, w_qkv, nq_cols, tm):
    m, d = h.shape
    n = w_qkv.shape[1]
    tn = 512
    kern = functools.partial(_qkv_kernel, nq_tiles=nq_cols // tn)
    return pl.pallas_call(
        kern,
        grid=(m // tm, n // tn),
        in_specs=[pl.BlockSpec((tm, d), lambda i, j: (i, 0)),
                  pl.BlockSpec((1, d), lambda i, j: (0, 0)),
                  pl.BlockSpec((1, d), lambda i, j: (0, 0)),
                  pl.BlockSpec((d, tn), lambda i, j: (0, j))],
        out_specs=pl.BlockSpec((tm, tn), lambda i, j: (i, j)),
        out_shape=jax.ShapeDtypeStruct((m, n), BF16),
        scratch_shapes=[pltpu.VMEM((tm, d), BF16), pltpu.VMEM((tm, d), BF16)],
        compiler_params=_params(("parallel", "arbitrary"), 48),
        name="qkv_proj",
    )(h, gq, gkv, w_qkv)


def _attn_kernel(q_ref, k_ref, v_ref, km_ref, vm_ref, o_ref, acc_ref, c_ref, *, scale):
    t = q_ref.shape[0]
    hd = SB_HEAD_DIM
    n_par = q_ref.shape[1] // hd
    qi = pl.program_id(2)
    ri = lax.broadcasted_iota(jnp.int32, (t, t), 0)
    ci = lax.broadcasted_iota(jnp.int32, (t, t), 1)
    upper = (ri > ci).astype(BF16)
    upper2 = jnp.concatenate([upper, upper], axis=0)

    def step(k_of, v_of, mask, u2):
        heads = [slice(g * hd, (g + 1) * hd) for g in range(n_par)]
        zs = [_dot_nt(q_ref[:, cols], k_of(cols)) * (scale * LOG2E) for cols in heads]
        ls2s, lk2s, hilos = [], [], []
        for z2 in zs:
            ls2 = jnp.minimum(z2, 0.0) - jnp.log(1.0 + jnp.exp2(-jnp.abs(z2))) * LOG2E
            lk2 = ls2 - z2
            if mask is not None:
                lk2 = jnp.where(mask, lk2, 0.0)
            hi = lk2.astype(BF16)
            lo = (lk2 - hi.astype(F32)).astype(BF16)
            ls2s.append(ls2)
            lk2s.append(lk2)
            hilos.append(jnp.concatenate([hi, lo], axis=1))
        tails = [_dot(hl, u2) for hl in hilos]
        cmax = None
        ps = []
        for g in range(n_par):
            c = c_ref[g]
            p = jnp.exp2(ls2s[g] + tails[g] + c)
            if mask is not None:
                p = jnp.where(mask, p, 0.0)
            ps.append(p.astype(BF16))
            c_new = c + jnp.sum(lk2s[g], axis=1, keepdims=True)
            c_ref[g] = c_new
            cmax = c_new if cmax is None else jnp.maximum(cmax, c_new)
        for g in range(n_par):
            acc_ref[g] += _dot(ps[g], v_of(heads[g]))
        return (jnp.max(cmax) > EXP_ZERO_F32 * LOG2E).astype(jnp.int32)

    acc_ref[...] = jnp.zeros_like(acc_ref)
    c_ref[...] = jnp.zeros_like(c_ref)

    def block(kb):
        s = pl.multiple_of(kb * t, t)
        return (lambda cols: k_ref[pl.ds(s, t), cols]), (lambda cols: v_ref[pl.ds(s, t), cols])

    go = step(*block(qi), ci < ri, upper2)

    def cond(carry):
        kb, alive = carry
        return jnp.logical_and(kb >= 0, alive > 0)

    def body(carry):
        kb, _ = carry
        return kb - 1, step(*block(kb), None, upper2)

    _, go = lax.while_loop(cond, body, (qi - 1, go))

    @pl.when(go > 0)
    def _():
        tm = km_ref.shape[0]
        mi = lax.broadcasted_iota(jnp.int32, (t, tm), 1)
        mr = lax.broadcasted_iota(jnp.int32, (tm, tm), 0)
        mc = lax.broadcasted_iota(jnp.int32, (tm, tm), 1)
        um = (mr > mc).astype(BF16)
        step(lambda cols: km_ref[:, cols], lambda cols: vm_ref[:, cols], mi < N_META,
             jnp.concatenate([um, um], axis=0))

    for g in range(n_par):
        o_ref[:, g * hd:(g + 1) * hd] = acc_ref[g].astype(o_ref.dtype)


def _attention(qkv, kv_meta, nb, seq, n_heads):
    t = ATTN_BLOCK
    n_par = ATTN_HEADS_PER_STEP
    w = n_par * SB_HEAD_DIM
    ng = n_heads // n_par
    kern = functools.partial(_attn_kernel, scale=SB_HEAD_DIM ** -0.5)
    qkv3 = qkv.reshape(nb, seq, qkv.shape[1])
    return pl.pallas_call(
        kern,
        grid=(nb, ng, seq // t),
        in_specs=[pl.BlockSpec((None, t, w), lambda b, h, i: (b, i, h)),
                  pl.BlockSpec((None, seq, w), lambda b, h, i: (b, 0, ng + h)),
                  pl.BlockSpec((None, seq, w), lambda b, h, i: (b, 0, 2 * ng + h)),
                  pl.BlockSpec((LANES, w), lambda b, h, i: (0, ng + h)),
                  pl.BlockSpec((LANES, w), lambda b, h, i: (0, 2 * ng + h))],
        out_specs=pl.BlockSpec((None, t, w), lambda b, h, i: (b, i, h)),
        out_shape=jax.ShapeDtypeStruct((nb, seq, n_heads * SB_HEAD_DIM), BF16),
        scratch_shapes=[pltpu.VMEM((n_par, t, SB_HEAD_DIM), F32), pltpu.VMEM((n_par, t, 1), F32)],
        compiler_params=_params(("parallel", "parallel", "arbitrary"), 48),
        name="sb_attention",
    )(qkv3, qkv3, qkv3, kv_meta, kv_meta)


def _row_tile(m):
    return 1024 if m % 1024 == 0 else m


def kernel(x, meta_tokens, norm_mix, norm_ffn, ssd_in_proj, ssd_conv_w, ssd_conv_b, ssd_dt_bias,
           ssd_a_log, ssd_d, ssd_norm, ssd_out_proj, kv_norm, w_kv, sb_w_q, sb_w_o, ffn_gate_up,
           ffn_down, final_norm):
    nb, seq, d = x.shape
    d_inner = ssd_out_proj.shape[1]
    conv_dim = ssd_conv_w.shape[2]
    n_ssd_heads = ssd_dt_bias.shape[1]
    sb_width = sb_w_q.shape[2]
    n_sb_heads = sb_width // SB_HEAD_DIM
    assert seq % ATTN_BLOCK == 0 and seq % CHUNK == 0 and n_ssd_heads <= LANES
    assert meta_tokens.shape[0] == N_META == HALO

    w_in = ssd_in_proj[:1].astype(BF16)
    w_dt = jnp.pad(ssd_in_proj[0][:, d_inner + conv_dim:], ((0, 0), (0, LANES - n_ssd_heads))).astype(BF16)
    w_out = ssd_out_proj[0].astype(BF16)
    w_qkv = jnp.concatenate([sb_w_q[0].astype(BF16), w_kv.astype(BF16)], axis=1)
    w_o = sb_w_o[0].astype(BF16)
    w_gu = ffn_gate_up.astype(BF16)
    w_dn = ffn_down.astype(BF16)
    row = lambda v: v.reshape(1, -1).astype(F32)
    d_exp = row(jnp.repeat(ssd_d[0], SSD_HEADDIM))
    conv_w = ssd_conv_w[0].astype(F32)
    conv_b = row(ssd_conv_b[0])
    ssd_consts = (row(ssd_dt_bias[0]), row(ssd_a_log[0]), d_exp, row(ssd_norm[0]))
    gw = d_inner // SSD_GROUPS

    hm = meta_tokens.astype(F32)
    zx_m, dt_m = _in_proj(hm, hm, row(norm_mix[0]), w_in, w_dt, conv_w, conv_b, N_META, None)
    lead = CHUNK - N_META
    g_m, st_m = _ssd(jnp.pad(zx_m, ((lead, 0), (0, 0))), jnp.pad(dt_m, ((lead, 0), (0, 0))),
                     *ssd_consts, jnp.zeros((SSD_GROUPS, SSD_STATE, gw), F32),
                     nb=1, nck=1, first_valid_row=lead)
    hm, um = _mm_res(g_m[lead:], w_out, hm, row(norm_ffn[0]), N_META, "out_proj_meta")
    hm = _ffn(hm, um, w_gu, w_dn, 0, row(final_norm), N_META, False, "ffn0_meta")
    qkv_m = _qkv(hm, row(norm_mix[1]), row(kv_norm), w_qkv, sb_width, N_META)
    kv_meta = jnp.pad(qkv_m, ((0, LANES - N_META), (0, 0)))

    tm = _row_tile(nb * seq)
    assert seq % tm == 0
    h = x.reshape(nb * seq, d).astype(F32)
    zx, dtr = _in_proj(h, meta_tokens.astype(F32), row(norm_mix[0]), w_in, w_dt, conv_w, conv_b, tm, seq)
    g, _ = _ssd(zx, dtr, *ssd_consts, st_m[0], nb=nb, nck=seq // CHUNK, first_valid_row=0)
    h, u = _mm_res(g, w_out, h, row(norm_ffn[0]), tm // 2, "out_proj")
    h = _ffn(h, u, w_gu, w_dn, 0, row(final_norm), tm // 2, False, "ffn0")

    qkv = _qkv(h, row(norm_mix[1]), row(kv_norm), w_qkv, sb_width, tm)
    o = _attention(qkv, kv_meta, nb, seq, n_sb_heads)
    h, u = _mm_res(o.reshape(nb * seq, sb_width), w_o, h, row(norm_ffn[1]), tm // 2, "o_proj")
    out = _ffn(h, u, w_gu, w_dn, 1, row(final_norm), tm // 2, True, "ffn1")
    return out.reshape(nb, seq, d)
```

```python
import functools

import jax
import jax.numpy as jnp
from jax import lax
from jax.experimental import pallas as pl
from jax.experimental.pallas import tpu as pltpu

F32 = jnp.float32
BF16 = jnp.bfloat16
EPS = 1e-6
CHUNK = 64
N_META = 16
HALO = 16
SSD_HEADDIM = 64
SSD_GROUPS = 8
SSD_STATE = 128
SSD_CONV = 4
SSD_GROUP_BATCH = 4
FFN_HIDDEN_TILE = 512
GATE_UP_SUB_ROWS = 512
PROJ_SUB_ROWS = 512
PROJ_COL_TILE = 1024
SB_HEAD_DIM = 128
LANES = 128
ATTN_BLOCK = 256
ATTN_HEADS_PER_STEP = 4
LOG2E = 1.4426950408889634
EXP_ZERO_F32 = -104.0
MIB = 1024 * 1024
HIGHEST = lax.Precision.HIGHEST


def _params(sem, vmem_mib):
    return pltpu.CompilerParams(dimension_semantics=sem, vmem_limit_bytes=vmem_mib * MIB)


def _dot(a, b):
    return jnp.dot(a, b, preferred_element_type=F32)


def _dot_nt(a, b):
    return lax.dot_general(a, b, (((1,), (1,)), ((), ())), preferred_element_type=F32)


def _dot_tn(a, b, precision=None):
    return lax.dot_general(a, b, (((0,), (0,)), ((), ())), precision=precision,
                           preferred_element_type=F32)


def _silu(x):
    hx = 0.5 * x
    return hx + hx * jnp.tanh(hx)


def _store_normed(x_ref, g_ref, u_ref):
    rows = x_ref.shape[0]
    slab = min(rows, 256)
    for r in range(0, rows, slab):
        x = x_ref[r:r + slab, :]
        ms = jnp.mean(x * x, axis=-1, keepdims=True)
        u_ref[r:r + slab, :] = (x * lax.rsqrt(ms + EPS) * g_ref[...]).astype(BF16)


def _in_proj_kernel(x_ref, xh_ref, xm_ref, g_ref, w_ref, wdt_ref, cw_ref, cb_ref,
                    o_ref, odt_ref, u_ref, uh_ref, *slab_refs, n_plain, tiles_per_seq):
    i = pl.program_id(0)
    j = pl.program_id(1)
    tm = x_ref.shape[0]
    tn = w_ref.shape[1]

    @pl.when(j == 0)
    def _():
        _store_normed(x_ref, g_ref, u_ref)
        odt_ref[...] = _dot(u_ref[...], wdt_ref[...])
        if tiles_per_seq is not None:
            @pl.when(i % tiles_per_seq == 0)
            def _():
                _store_normed(xm_ref, g_ref, uh_ref)

            @pl.when(i % tiles_per_seq != 0)
            def _():
                _store_normed(xh_ref, g_ref, uh_ref)

    @pl.when(j < n_plain)
    def _():
        o_ref[...] = _dot(u_ref[...], w_ref[...]).astype(o_ref.dtype)

    @pl.when(j >= n_plain)
    def _():
        lane_tiles = [slice(t * LANES, (t + 1) * LANES) for t in range(tn // LANES)]
        sub = slab_refs[0].shape[1] - HALO
        if tiles_per_seq is not None:
            ctx = _dot(uh_ref[...], w_ref[...])
        else:
            ctx = jnp.zeros((HALO, tn), F32)
        for b, slab_ref in enumerate(slab_refs):
            raw = _dot(u_ref[b * sub:(b + 1) * sub, :], w_ref[...])
            for t, cols in enumerate(lane_tiles):
                slab_ref[t, 0:HALO, :] = ctx[:, cols]
                slab_ref[t, HALO:HALO + sub, :] = raw[:, cols]
            ctx = raw[sub - HALO:sub, :]
        for b, slab_ref in enumerate(slab_refs):
            for t, cols in enumerate(lane_tiles):
                acc = cb_ref[:, cols]
                for k in range(SSD_CONV):
                    s = HALO - (SSD_CONV - 1) + k
                    acc = acc + cw_ref[k:k + 1, cols] * slab_ref[t, s:s + sub, :]
                o_ref[b * sub:(b + 1) * sub, cols] = _silu(acc).astype(o_ref.dtype)


def _in_proj(x2d, x_meta, gain, w_in, w_dt, conv_w, conv_b, tm, seq):
    m, k = x2d.shape
    tn = PROJ_COL_TILE
    n = (w_in.shape[2] // tn) * tn
    n_plain = (n - conv_w.shape[1]) // tn
    assert n == n_plain * tn + conv_w.shape[1]
    kern = functools.partial(_in_proj_kernel, n_plain=n_plain,
                             tiles_per_seq=None if seq is None else seq // tm)
    conv_col = lambda i, j: (0, jnp.maximum(j - n_plain, 0))
    sub = min(tm, PROJ_SUB_ROWS)
    return pl.pallas_call(
        kern,
        grid=(m // tm, n // tn),
        in_specs=[
            pl.BlockSpec((tm, k), lambda i, j: (i, 0)),
            pl.BlockSpec((HALO, k), lambda i, j: (jnp.maximum(i * (tm // HALO) - 1, 0), 0)),
            pl.BlockSpec((HALO, k), lambda i, j: (0, 0)),
            pl.BlockSpec((1, k), lambda i, j: (0, 0)),
            pl.BlockSpec((None, k, tn), lambda i, j: (0, 0, j)),
            pl.BlockSpec((k, LANES), lambda i, j: (0, 0)),
            pl.BlockSpec((SSD_CONV, tn), conv_col),
            pl.BlockSpec((1, tn), conv_col),
        ],
        out_specs=[
            pl.BlockSpec((tm, tn), lambda i, j: (i, j)),
            pl.BlockSpec((tm, LANES), lambda i, j: (i, 0)),
        ],
        out_shape=[jax.ShapeDtypeStruct((m, n), BF16), jax.ShapeDtypeStruct((m, LANES), F32)],
        scratch_shapes=[pltpu.VMEM((tm, k), BF16), pltpu.VMEM((HALO, k), BF16)]
        + [pltpu.VMEM((tn // LANES, HALO + sub, LANES), F32)] * (tm // sub),
        compiler_params=_params(("parallel", "arbitrary"), 48),
        name="in_proj",
    )(x2d, x2d, x_meta, gain, w_in, w_dt, conv_w, conv_b)


def _ssd_kernel(z0_ref, z1_ref, xa_ref, xb_ref, bc_ref, dt_ref, dtb_ref,
                alog_ref, dsk_ref, ng_ref, st0_ref, sel_ref,
                g_ref, stout_ref, state_ref, *, first_valid_row):
    c = pl.program_id(1)
    q_len = CHUNK
    d_inner = g_ref.shape[1]
    gw = d_inner // SSD_GROUPS
    half = xa_ref.shape[1]
    n_bc = SSD_GROUPS * SSD_STATE

    @pl.when(c == 0)
    def _():
        state_ref[...] = st0_ref[...]

    row = lax.broadcasted_iota(jnp.int32, (q_len, 1), 0)
    valid = row >= first_valid_row

    nh = dtb_ref.shape[1]
    dtr = dt_ref[:, 0:nh] + dtb_ref[...]
    dt = jnp.maximum(dtr, 0.0) + jnp.log1p(jnp.exp(-jnp.abs(dtr)))
    if first_valid_row:
        dt = jnp.where(valid, dt, 0.0)
    dta = dt * (-jnp.exp(alog_ref[...]))

    ri = lax.broadcasted_iota(jnp.int32, (q_len, q_len), 0)
    ci = lax.broadcasted_iota(jnp.int32, (q_len, q_len), 1)
    tri_incl = (ri >= ci).astype(F32)
    eye = (ri == ci).astype(F32)
    cum = jnp.dot(tri_incl, dta, precision=HIGHEST, preferred_element_type=F32)
    cum_t = _dot_tn(dta, (ri <= ci).astype(F32), precision=HIGHEST)
    dt_t = _dot_tn(dt, eye, precision=HIGHEST)
    w_in = jnp.exp(cum[q_len - 1:q_len, :] - cum) * dt

    lane = lax.broadcasted_iota(jnp.int32, (q_len, LANES), 1)
    rowq = lax.broadcasted_iota(jnp.int32, (q_len, LANES), 0)
    left = lane < SSD_HEADDIM
    causal2 = rowq >= jnp.where(left, lane, lane - SSD_HEADDIM)
    rr = lax.broadcasted_iota(jnp.int32, (2 * q_len, LANES), 0)
    ll = lax.broadcasted_iota(jnp.int32, (2 * q_len, LANES), 1)
    blockdiag = (rr < q_len) == (ll < SSD_HEADDIM)

    cum_hi = cum.astype(BF16)
    cum_r = cum - cum_hi.astype(F32)
    cum_mid = cum_r.astype(BF16)
    cum_lo = (cum_r - cum_mid.astype(F32)).astype(BF16)
    w16 = w_in.astype(BF16)
    zero16 = jnp.zeros_like(w16)
    spread_lhs = jnp.concatenate(
        [jnp.concatenate([cum_hi, cum_mid, cum_lo], axis=1),
         jnp.concatenate([w16, zero16, zero16], axis=1)], axis=0)

    def pair_rows(a, h):
        return jnp.concatenate([a[h:h + 1, :], a[h + 1:h + 2, :]], axis=1)

    heads_per_group = gw // SSD_HEADDIM
    pairs = heads_per_group // 2

    def x_pair(g, p):
        xr = xa_ref if (g * gw) < half else xb_ref
        lo = (g * gw) % half + p * LANES
        return xr[:, lo:lo + LANES].astype(F32)

    def run_groups(groups):
        bms, cb2s, y_offs, spreads = {}, {}, {}, {}
        for g in groups:
            bm = bc_ref[:, g * SSD_STATE:(g + 1) * SSD_STATE].astype(F32)
            cm16 = bc_ref[:, n_bc + g * SSD_STATE:n_bc + (g + 1) * SSD_STATE].astype(BF16)
            bms[g] = bm
            cb2s[g] = _dot_nt(cm16, jnp.concatenate([bm, bm], axis=0).astype(BF16))
            y_offs[g] = _dot(cm16, state_ref[g].astype(BF16))
            spreads[g] = _dot(spread_lhs, sel_ref[:, g * gw:(g + 1) * gw])

        m2s, xbds, ea2s = {}, {}, {}
        for g in groups:
            for p in range(pairs):
                h = g * heads_per_group + 2 * p
                a2 = spreads[g][0:q_len, p * LANES:(p + 1) * LANES]
                seg = a2 - pair_rows(cum_t, h)
                m2 = cb2s[g] * jnp.exp(jnp.where(causal2, seg, -jnp.inf)) * pair_rows(dt_t, h)
                xs2 = x_pair(g, p)
                m2s[g, p] = m2.astype(BF16)
                xbds[g, p] = jnp.where(blockdiag, jnp.concatenate([xs2, xs2], axis=0), 0.0).astype(BF16)
                ea2s[g, p] = jnp.exp(a2)

        y_diags = {gp: _dot(m2s[gp], xbds[gp]) for gp in m2s}

        for g in groups:
            ys, xws, decs = [], [], []
            for p in range(pairs):
                sl = slice(p * LANES, (p + 1) * LANES)
                xs2 = x_pair(g, p)
                ys.append(y_diags[g, p] + y_offs[g][:, sl] * ea2s[g, p]
                          + xs2 * dsk_ref[:, g * gw + p * LANES:g * gw + (p + 1) * LANES])
                xws.append((xs2 * spreads[g][q_len:2 * q_len, sl]).astype(BF16))
                decs.append(ea2s[g, p][q_len - 1:q_len, :])
            xw = jnp.concatenate(xws, axis=1)
            dec = jnp.concatenate(decs, axis=1)
            state_ref[g] = state_ref[g] * dec + _dot(bms[g].T.astype(BF16), xw)
            y = jnp.concatenate(ys, axis=1)
            zr = z0_ref if (g * gw) < half else z1_ref
            zc = (g * gw) % half
            gt = y * _silu(zr[:, zc:zc + gw].astype(F32))
            ms = jnp.mean(gt * gt, axis=-1, keepdims=True)
            g_ref[:, g * gw:(g + 1) * gw] = (gt * lax.rsqrt(ms + EPS) * ng_ref[:, g * gw:(g + 1) * gw]).astype(g_ref.dtype)

    for g0 in range(0, SSD_GROUPS, SSD_GROUP_BATCH):
        run_groups(range(g0, g0 + SSD_GROUP_BATCH))

    @pl.when(c == pl.num_programs(1) - 1)
    def _():
        stout_ref[...] = state_ref[...]


def _ssd(zx, dtraw, dt_bias, a_log, d_exp, norm_g, st0, *, nb, nck, first_valid_row):
    d_inner = norm_g.shape[1]
    half = 2 * SSD_GROUPS * SSD_STATE
    assert d_inner == 2 * half and zx.shape[1] == 2 * d_inner + half
    gw = d_inner // SSD_GROUPS
    nh = dt_bias.shape[1]
    sel = ((jnp.arange(3 * nh)[:, None] % nh) == (jnp.arange(d_inner)[None, :] // SSD_HEADDIM)).astype(BF16)

    def rows(col):
        return pl.BlockSpec((CHUNK, half), lambda b, c: (b * nck + c, col))

    def full(a):
        nd = a.ndim
        return pl.BlockSpec(a.shape, lambda b, c: (0,) * nd)

    kern = functools.partial(_ssd_kernel, first_valid_row=first_valid_row)
    return pl.pallas_call(
        kern,
        grid=(nb, nck),
        in_specs=[rows(0), rows(1), rows(2), rows(3), rows(4),
                  pl.BlockSpec((CHUNK, LANES), lambda b, c: (b * nck + c, 0)),
                  full(dt_bias), full(a_log), full(d_exp), full(norm_g), full(st0), full(sel)],
        out_specs=[pl.BlockSpec((CHUNK, d_inner), lambda b, c: (b * nck + c, 0)),
                   pl.BlockSpec((None, SSD_GROUPS, SSD_STATE, gw), lambda b, c: (b, 0, 0, 0))],
        out_shape=[jax.ShapeDtypeStruct((nb * nck * CHUNK, d_inner), BF16),
                   jax.ShapeDtypeStruct((nb, SSD_GROUPS, SSD_STATE, gw), F32)],
        scratch_shapes=[pltpu.VMEM((SSD_GROUPS, SSD_STATE, gw), F32)],
        compiler_params=_params(("parallel", "arbitrary"), 32),
        name="ssd_scan",
    )(zx, zx, zx, zx, zx, dtraw, dt_bias, a_log, d_exp, norm_g, st0, sel)


def _mm_res_kernel(a_ref, w_ref, h_ref, *rest, emit):
    o_ref = rest[-1] if emit != "h_and_u" else rest[-2]
    rows = a_ref.shape[0]
    sub = min(rows, 256)
    prods = [_dot(a_ref[r:r + sub, :], w_ref[...]) for r in range(0, rows, sub)]
    for idx, prod in enumerate(prods):
        rs = slice(idx * sub, (idx + 1) * sub)
        hn = h_ref[rs, :] + prod
        if emit == "h":
            o_ref[rs, :] = hn
            continue
        normed = hn * lax.rsqrt(jnp.mean(hn * hn, axis=-1, keepdims=True) + EPS) * rest[0][...]
        if emit == "normed":
            o_ref[rs, :] = normed
        else:
            o_ref[rs, :] = hn
            rest[-1][rs, :] = normed.astype(BF16)


def _mm_res(a, w, layer, h, tm, name, emit="h", gain=None):
    assert emit in ("h", "h_and_u", "normed") and (gain is None) == (emit == "h")
    m, k = a.shape
    n = w.shape[2]
    row_spec = pl.BlockSpec((tm, n), lambda i: (i, 0))
    in_specs = [pl.BlockSpec((tm, k), lambda i: (i, 0)),
                pl.BlockSpec((None, k, n), lambda i: (layer, 0, 0), pipeline_mode=pl.Buffered(1)),
                row_spec]
    args = [a, w, h]
    if gain is not None:
        in_specs.append(pl.BlockSpec((1, n), lambda i: (0, 0)))
        args.append(gain)
    out_specs = [row_spec]
    out_shape = [jax.ShapeDtypeStruct((m, n), F32)]
    if emit == "h_and_u":
        out_specs.append(row_spec)
        out_shape.append(jax.ShapeDtypeStruct((m, n), BF16))
    res = pl.pallas_call(
        functools.partial(_mm_res_kernel, emit=emit),
        grid=(m // tm,),
        in_specs=in_specs,
        out_specs=out_specs,
        out_shape=out_shape,
        compiler_params=_params(("parallel",), 56),
        name=name,
    )(*args)
    return res if emit == "h_and_u" else res[0]


def _gate_up_kernel(u_ref, wg_ref, wu_ref, o_ref):
    rows = u_ref.shape[0]
    sub = min(rows, GATE_UP_SUB_ROWS)
    parts = []
    for r in range(0, rows, sub):
        u = u_ref[r:r + sub, :]
        parts.append((_dot(u, wg_ref[...]), _dot(u, wu_ref[...])))
    for idx, (gate, up) in enumerate(parts):
        o_ref[idx * sub:(idx + 1) * sub, :] = (_silu(gate) * up).astype(o_ref.dtype)


def _gate_up(u, w_gate_up, layer, tm, name):
    m, d = u.shape
    f = w_gate_up.shape[2] // 2
    tf = FFN_HIDDEN_TILE
    nf = f // tf
    return pl.pallas_call(
        _gate_up_kernel,
        grid=(m // tm, nf),
        in_specs=[pl.BlockSpec((tm, d), lambda i, j: (i, 0)),
                  pl.BlockSpec((None, d, tf), lambda i, j: (layer, 0, j)),
                  pl.BlockSpec((None, d, tf), lambda i, j: (layer, 0, nf + j))],
        out_specs=pl.BlockSpec((tm, tf), lambda i, j: (i, j)),
        out_shape=jax.ShapeDtypeStruct((m, f), BF16),
        compiler_params=_params(("parallel", "arbitrary"), 48),
        name=name,
    )(u, w_gate_up, w_gate_up)


def _qkv_kernel(h_ref, gq_ref, gkv_ref, w_ref, o_ref, uq_ref, ukv_ref, *, nq_tiles):
    j = pl.program_id(1)

    @pl.when(j == 0)
    def _():
        rows = h_ref.shape[0]
        slab = min(rows, 256)
        for r in range(0, rows, slab):
            x = h_ref[r:r + slab, :]
            xn = x * lax.rsqrt(jnp.mean(x * x, axis=-1, keepdims=True) + EPS)
            uq_ref[r:r + slab, :] = (xn * gq_ref[...]).astype(BF16)
            ukv_ref[r:r + slab, :] = (xn * gkv_ref[...]).astype(BF16)

    @pl.when(j < nq_tiles)
    def _():
        o_ref[...] = _dot(uq_ref[...], w_ref[...]).astype(o_ref.dtype)

    @pl.when(j >= nq_tiles)
    def _():
        o_ref[...] = _dot(ukv_ref[...], w_ref[...]).astype(o_ref.dtype)


def _qkv(h, gq, gkv, w_qkv, nq_cols, tm):
    m, d = h.shape
    n = w_qkv.shape[1]
    tn = PROJ_COL_TILE
    kern = functools.partial(_qkv_kernel, nq_tiles=nq_cols // tn)
    return pl.pallas_call(
        kern,
        grid=(m // tm, n // tn),
        in_specs=[pl.BlockSpec((tm, d), lambda i, j: (i, 0)),
                  pl.BlockSpec((1, d), lambda i, j: (0, 0)),
                  pl.BlockSpec((1, d), lambda i, j: (0, 0)),
                  pl.BlockSpec((d, tn), lambda i, j: (0, j))],
        out_specs=pl.BlockSpec((tm, tn), lambda i, j: (i, j)),
        out_shape=jax.ShapeDtypeStruct((m, n), BF16),
        scratch_shapes=[pltpu.VMEM((tm, d), BF16), pltpu.VMEM((tm, d), BF16)],
        compiler_params=_params(("parallel", "arbitrary"), 48),
        name="qkv_proj",
    )(h, gq, gkv, w_qkv)


def _attn_kernel(q_ref, k_ref, v_ref, km_ref, vm_ref, o_ref, acc_ref, c_ref, *, scale):
    t = q_ref.shape[0]
    hd = SB_HEAD_DIM
    n_par = q_ref.shape[1] // hd
    qi = pl.program_id(2)
    ri = lax.broadcasted_iota(jnp.int32, (t, t), 0)
    ci = lax.broadcasted_iota(jnp.int32, (t, t), 1)
    upper = (ri > ci).astype(BF16)
    upper2 = jnp.concatenate([upper, upper], axis=0)

    def step(k_of, v_of, mask, u2):
        heads = [slice(g * hd, (g + 1) * hd) for g in range(n_par)]
        zs = [_dot_nt(q_ref[:, cols], k_of(cols)) * (scale * LOG2E) for cols in heads]
        ls2s, lk2s, hilos = [], [], []
        for z2 in zs:
            ls2 = jnp.minimum(z2, 0.0) - jnp.log(1.0 + jnp.exp2(-jnp.abs(z2))) * LOG2E
            lk2 = ls2 - z2
            if mask is not None:
                lk2 = jnp.where(mask, lk2, 0.0)
            hi = lk2.astype(BF16)
            lo = (lk2 - hi.astype(F32)).astype(BF16)
            ls2s.append(ls2)
            lk2s.append(lk2)
            hilos.append(jnp.concatenate([hi, lo], axis=1))
        tails = [_dot(hl, u2) for hl in hilos]
        cmax = None
        ps = []
        for g in range(n_par):
            c = c_ref[g]
            p = jnp.exp2(ls2s[g] + tails[g] + c)
            if mask is not None:
                p = jnp.where(mask, p, 0.0)
            ps.append(p.astype(BF16))
            c_new = c + jnp.sum(lk2s[g], axis=1, keepdims=True)
            c_ref[g] = c_new
            cmax = c_new if cmax is None else jnp.maximum(cmax, c_new)
        for g in range(n_par):
            acc_ref[g] += _dot(ps[g], v_of(heads[g]))
        return (jnp.max(cmax) > EXP_ZERO_F32 * LOG2E).astype(jnp.int32)

    acc_ref[...] = jnp.zeros_like(acc_ref)
    c_ref[...] = jnp.zeros_like(c_ref)

    def block(kb):
        s = pl.multiple_of(kb * t, t)
        return (lambda cols: k_ref[pl.ds(s, t), cols]), (lambda cols: v_ref[pl.ds(s, t), cols])

    go = step(*block(qi), ci < ri, upper2)

    def cond(carry):
        kb, alive = carry
        return jnp.logical_and(kb >= 0, alive > 0)

    def body(carry):
        kb, _ = carry
        return kb - 1, step(*block(kb), None, upper2)

    _, go = lax.while_loop(cond, body, (qi - 1, go))

    @pl.when(go > 0)
    def _():
        tm = km_ref.shape[0]
        mi = lax.broadcasted_iota(jnp.int32, (t, tm), 1)
        mr = lax.broadcasted_iota(jnp.int32, (tm, tm), 0)
        mc = lax.broadcasted_iota(jnp.int32, (tm, tm), 1)
        um = (mr > mc).astype(BF16)
        step(lambda cols: km_ref[:, cols], lambda cols: vm_ref[:, cols], mi < N_META,
             jnp.concatenate([um, um], axis=0))

    for g in range(n_par):
        o_ref[:, g * hd:(g + 1) * hd] = acc_ref[g].astype(o_ref.dtype)


def _attention(qkv, kv_meta, nb, seq, n_heads):
    t = ATTN_BLOCK
    n_par = ATTN_HEADS_PER_STEP
    w = n_par * SB_HEAD_DIM
    ng = n_heads // n_par
    kern = functools.partial(_attn_kernel, scale=SB_HEAD_DIM ** -0.5)
    qkv3 = qkv.reshape(nb, seq, qkv.shape[1])
    return pl.pallas_call(
        kern,
        grid=(nb, ng, seq // t),
        in_specs=[pl.BlockSpec((None, t, w), lambda b, h, i: (b, i, h)),
                  pl.BlockSpec((None, seq, w), lambda b, h, i: (b, 0, ng + h)),
                  pl.BlockSpec((None, seq, w), lambda b, h, i: (b, 0, 2 * ng + h)),
                  pl.BlockSpec((LANES, w), lambda b, h, i: (0, ng + h)),
                  pl.BlockSpec((LANES, w), lambda b, h, i: (0, 2 * ng + h))],
        out_specs=pl.BlockSpec((None, t, w), lambda b, h, i: (b, i, h)),
        out_shape=jax.ShapeDtypeStruct((nb, seq, n_heads * SB_HEAD_DIM), BF16),
        scratch_shapes=[pltpu.VMEM((n_par, t, SB_HEAD_DIM), F32), pltpu.VMEM((n_par, t, 1), F32)],
        compiler_params=_params(("parallel", "parallel", "arbitrary"), 48),
        name="sb_attention",
    )(qkv3, qkv3, qkv3, kv_meta, kv_meta)


def _row_tile(m):
    return 1024 if m % 1024 == 0 else m


def kernel(x, meta_tokens, norm_mix, norm_ffn, ssd_in_proj, ssd_conv_w, ssd_conv_b, ssd_dt_bias,
           ssd_a_log, ssd_d, ssd_norm, ssd_out_proj, kv_norm, w_kv, sb_w_q, sb_w_o, ffn_gate_up,
           ffn_down, final_norm):
    nb, seq, d = x.shape
    d_inner = ssd_out_proj.shape[1]
    conv_dim = ssd_conv_w.shape[2]
    n_ssd_heads = ssd_dt_bias.shape[1]
    sb_width = sb_w_q.shape[2]
    n_sb_heads = sb_width // SB_HEAD_DIM
    assert seq % ATTN_BLOCK == 0 and seq % CHUNK == 0 and n_ssd_heads <= LANES
    assert meta_tokens.shape[0] == N_META == HALO

    w_in = ssd_in_proj[:1].astype(BF16)
    w_dt = jnp.pad(ssd_in_proj[0][:, d_inner + conv_dim:], ((0, 0), (0, LANES - n_ssd_heads))).astype(BF16)
    w_out = ssd_out_proj.astype(BF16)
    w_qkv = jnp.concatenate([sb_w_q[0].astype(BF16), w_kv.astype(BF16)], axis=1)
    w_o = sb_w_o.astype(BF16)
    w_gu = ffn_gate_up.astype(BF16)
    w_dn = ffn_down.astype(BF16)
    row = lambda v: v.reshape(1, -1).astype(F32)
    d_exp = row(jnp.repeat(ssd_d[0], SSD_HEADDIM))
    conv_w = ssd_conv_w[0].astype(F32)
    conv_b = row(ssd_conv_b[0])
    ssd_consts = (row(ssd_dt_bias[0]), row(ssd_a_log[0]), d_exp, row(ssd_norm[0]))
    gw = d_inner // SSD_GROUPS

    hm = meta_tokens.astype(F32)
    zx_m, dt_m = _in_proj(hm, hm, row(norm_mix[0]), w_in, w_dt, conv_w, conv_b, N_META, None)
    lead = CHUNK - N_META
    g_m, st_m = _ssd(jnp.pad(zx_m, ((lead, 0), (0, 0))), jnp.pad(dt_m, ((lead, 0), (0, 0))),
                     *ssd_consts, jnp.zeros((SSD_GROUPS, SSD_STATE, gw), F32),
                     nb=1, nck=1, first_valid_row=lead)
    hm, um = _mm_res(g_m[lead:], w_out, 0, hm, N_META, "out_proj_meta", "h_and_u", row(norm_ffn[0]))
    hm = _mm_res(_gate_up(um, w_gu, 0, N_META, "ffn0_up_meta"), w_dn, 0, hm, N_META, "ffn0_down_meta")
    qkv_m = _qkv(hm, row(norm_mix[1]), row(kv_norm), w_qkv, sb_width, N_META)
    kv_meta = jnp.pad(qkv_m, ((0, LANES - N_META), (0, 0)))

    tm = _row_tile(nb * seq)
    assert seq % tm == 0
    h = x.reshape(nb * seq, d).astype(F32)
    zx, dtr = _in_proj(h, meta_tokens.astype(F32), row(norm_mix[0]), w_in, w_dt, conv_w, conv_b, tm, seq)
    g, _ = _ssd(zx, dtr, *ssd_consts, st_m[0], nb=nb, nck=seq // CHUNK, first_valid_row=0)
    tm_up = 2 * tm if (nb * seq) % (2 * tm) == 0 else tm
    h, u = _mm_res(g, w_out, 0, h, tm // 4, "out_proj", "h_and_u", row(norm_ffn[0]))
    h = _mm_res(_gate_up(u, w_gu, 0, tm_up, "ffn0_up"), w_dn, 0, h, tm // 4, "ffn0_down")

    qkv = _qkv(h, row(norm_mix[1]), row(kv_norm), w_qkv, sb_width, tm)
    o = _attention(qkv, kv_meta, nb, seq, n_sb_heads)
    h, u = _mm_res(o.reshape(nb * seq, sb_width), w_o, 0, h, tm // 2, "o_proj", "h_and_u", row(norm_ffn[1]))
    out = _mm_res(_gate_up(u, w_gu, 1, tm_up, "ffn1_up"), w_dn, 1, h, tm // 4, "ffn1_down", "normed",
                  row(final_norm))
    return out.reshape(nb, seq, d)
```

```python
import functools

import jax
import jax.numpy as jnp
from jax import lax
from jax.experimental import pallas as pl
from jax.experimental.pallas import tpu as pltpu

F32 = jnp.float32
BF16 = jnp.bfloat16
EPS = 1e-6
CHUNK = 64
N_META = 16
SSD_HEADDIM = 64
SSD_GROUPS = 8
SSD_STATE = 128
SSD_CONV = 4
SSD_GROUP_BATCH = 4
FFN_HIDDEN_TILE = 512
GATE_UP_SUB_ROWS = 512
PROJ_COL_TILE = 1024
SB_HEAD_DIM = 128
LANES = 128
ATTN_BLOCK = 256
ATTN_HEADS_PER_STEP = 4
LOG2E = 1.4426950408889634
EXP_ZERO_F32 = -104.0
MIB = 1024 * 1024
HIGHEST = lax.Precision.HIGHEST


def _params(sem, vmem_mib):
    return pltpu.CompilerParams(dimension_semantics=sem, vmem_limit_bytes=vmem_mib * MIB)


def _dot(a, b):
    return jnp.dot(a, b, preferred_element_type=F32)


def _dot_nt(a, b):
    return lax.dot_general(a, b, (((1,), (1,)), ((), ())), preferred_element_type=F32)


def _dot_tn(a, b, precision=None):
    return lax.dot_general(a, b, (((0,), (0,)), ((), ())), precision=precision,
                           preferred_element_type=F32)


def _silu(x):
    hx = 0.5 * x
    return hx + hx * jnp.tanh(hx)


def _store_normed(x_ref, g_ref, u_ref):
    rows = x_ref.shape[0]
    slab = min(rows, 256)
    for r in range(0, rows, slab):
        x = x_ref[r:r + slab, :]
        ms = jnp.mean(x * x, axis=-1, keepdims=True)
        u_ref[r:r + slab, :] = (x * lax.rsqrt(ms + EPS) * g_ref[...]).astype(BF16)


def _in_proj_kernel(x_ref, g_ref, w_ref, wdt_ref, o_ref, odt_ref, u_ref):
    @pl.when(pl.program_id(1) == 0)
    def _():
        _store_normed(x_ref, g_ref, u_ref)
        odt_ref[...] = _dot(u_ref[...], wdt_ref[...])

    o_ref[...] = _dot(u_ref[...], w_ref[...]).astype(o_ref.dtype)


def _in_proj(x2d, gain, w_in, w_dt, tm):
    m, k = x2d.shape
    tn = PROJ_COL_TILE
    n = (w_in.shape[2] // tn) * tn
    return pl.pallas_call(
        _in_proj_kernel,
        grid=(m // tm, n // tn),
        in_specs=[
            pl.BlockSpec((tm, k), lambda i, j: (i, 0)),
            pl.BlockSpec((1, k), lambda i, j: (0, 0)),
            pl.BlockSpec((None, k, tn), lambda i, j: (0, 0, j)),
            pl.BlockSpec((k, LANES), lambda i, j: (0, 0)),
        ],
        out_specs=[
            pl.BlockSpec((tm, tn), lambda i, j: (i, j)),
            pl.BlockSpec((tm, LANES), lambda i, j: (i, 0)),
        ],
        out_shape=[jax.ShapeDtypeStruct((m, n), BF16), jax.ShapeDtypeStruct((m, LANES), F32)],
        scratch_shapes=[pltpu.VMEM((tm, k), BF16)],
        compiler_params=_params(("parallel", "arbitrary"), 48),
        name="in_proj",
    )(x2d, gain, w_in, w_dt)


def _ssd_kernel(z0_ref, z1_ref, xa_ref, xb_ref, bc_ref, dt_ref, cw_ref, cb_ref, dtb_ref,
                alog_ref, dsk_ref, ng_ref, st0_ref, tail0_ref, sel_ref,
                g_ref, stout_ref, state_ref, xext_ref, *, first_valid_row):
    c = pl.program_id(1)
    q_len = CHUNK
    d_inner = g_ref.shape[1]
    gw = d_inner // SSD_GROUPS
    half = xa_ref.shape[1]
    tiles_per_piece = half // LANES
    b_tile0 = d_inner // LANES
    c_tile0 = b_tile0 + SSD_GROUPS * SSD_STATE // LANES

    @pl.when(c == 0)
    def _():
        state_ref[...] = st0_ref[...]
        xext_ref[:, 0:8, :] = tail0_ref[...]

    for piece, ref in enumerate((xa_ref, xb_ref, bc_ref)):
        for t in range(tiles_per_piece):
            xext_ref[piece * tiles_per_piece + t, 8:8 + q_len, :] = ref[:, t * LANES:(t + 1) * LANES].astype(F32)

    row = lax.broadcasted_iota(jnp.int32, (q_len, 1), 0)
    valid = row >= first_valid_row

    def conv_act(t):
        cols = slice(t * LANES, (t + 1) * LANES)
        acc = cb_ref[:, cols]
        for k in range(SSD_CONV):
            s = 8 - (SSD_CONV - 1) + k
            acc = acc + cw_ref[k:k + 1, cols] * xext_ref[t, s:s + q_len, :]
        a = _silu(acc)
        return jnp.where(valid, a, 0.0) if first_valid_row else a

    nh = dtb_ref.shape[1]
    dtr = dt_ref[:, 0:nh] + dtb_ref[...]
    dt = jnp.maximum(dtr, 0.0) + jnp.log1p(jnp.exp(-jnp.abs(dtr)))
    if first_valid_row:
        dt = jnp.where(valid, dt, 0.0)
    dta = dt * (-jnp.exp(alog_ref[...]))

    def split3(a):
        hi = a.astype(BF16)
        r = a - hi.astype(F32)
        mid = r.astype(BF16)
        return hi, mid, (r - mid.astype(F32)).astype(BF16)

    ri = lax.broadcasted_iota(jnp.int32, (q_len, q_len), 0)
    ci = lax.broadcasted_iota(jnp.int32, (q_len, q_len), 1)
    rep3 = lambda m: jnp.concatenate([m.astype(BF16)] * 3, axis=0)
    dta3 = jnp.concatenate(split3(dta), axis=0)
    cum = _dot_tn(rep3(ri <= ci), dta3)
    cum_t = _dot_tn(dta3, rep3(ri <= ci))
    dt_t = _dot_tn(jnp.concatenate(split3(dt), axis=0), rep3(ri == ci))
    w_in = jnp.exp(cum[q_len - 1:q_len, :] - cum) * dt

    lane = lax.broadcasted_iota(jnp.int32, (q_len, LANES), 1)
    rowq = lax.broadcasted_iota(jnp.int32, (q_len, LANES), 0)
    left = lane < SSD_HEADDIM
    causal2 = rowq >= jnp.where(left, lane, lane - SSD_HEADDIM)
    rr = lax.broadcasted_iota(jnp.int32, (2 * q_len, LANES), 0)
    ll = lax.broadcasted_iota(jnp.int32, (2 * q_len, LANES), 1)
    blockdiag = (rr < q_len) == (ll < SSD_HEADDIM)

    w16 = w_in.astype(BF16)
    zero16 = jnp.zeros_like(w16)
    spread_lhs = jnp.concatenate(
        [jnp.concatenate(split3(cum), axis=1),
         jnp.concatenate([w16, zero16, zero16], axis=1)], axis=0)

    def pair_rows(a, h):
        return jnp.concatenate([a[h:h + 1, :], a[h + 1:h + 2, :]], axis=1)

    heads_per_group = gw // SSD_HEADDIM
    pairs = heads_per_group // 2

    def run_groups(groups):
        bms, cb2s, y_offs, spreads, xs = {}, {}, {}, {}, {}
        for g in groups:
            bm = conv_act(b_tile0 + g)
            cm16 = conv_act(c_tile0 + g).astype(BF16)
            bms[g] = bm
            for p in range(pairs):
                xs[g, p] = conv_act(g * pairs + p)
            cb2s[g] = _dot_nt(cm16, jnp.concatenate([bm, bm], axis=0).astype(BF16))
            y_offs[g] = _dot(cm16, state_ref[g].astype(BF16))
            spreads[g] = _dot(spread_lhs, sel_ref[:, g * gw:(g + 1) * gw])

        m2s, xbds, ea2s = {}, {}, {}
        for g in groups:
            for p in range(pairs):
                h = g * heads_per_group + 2 * p
                a2 = spreads[g][0:q_len, p * LANES:(p + 1) * LANES]
                seg = a2 - pair_rows(cum_t, h)
                m2 = cb2s[g] * jnp.exp(jnp.where(causal2, seg, -jnp.inf)) * pair_rows(dt_t, h)
                xs2 = xs[g, p]
                m2s[g, p] = m2.astype(BF16)
                xbds[g, p] = jnp.where(blockdiag, jnp.concatenate([xs2, xs2], axis=0), 0.0).astype(BF16)
                ea2s[g, p] = jnp.exp(a2)

        y_diags = {gp: _dot(m2s[gp], xbds[gp]) for gp in m2s}

        for g in groups:
            ys, xws, decs = [], [], []
            for p in range(pairs):
                sl = slice(p * LANES, (p + 1) * LANES)
                xs2 = xs[g, p]
                ys.append(y_diags[g, p] + y_offs[g][:, sl] * ea2s[g, p]
                          + xs2 * dsk_ref[:, g * gw + p * LANES:g * gw + (p + 1) * LANES])
                xws.append((xs2 * spreads[g][q_len:2 * q_len, sl]).astype(BF16))
                decs.append(ea2s[g, p][q_len - 1:q_len, :])
            xw = jnp.concatenate(xws, axis=1)
            dec = jnp.concatenate(decs, axis=1)
            state_ref[g] = state_ref[g] * dec + _dot(bms[g].T.astype(BF16), xw)
            y = jnp.concatenate(ys, axis=1)
            zr = z0_ref if (g * gw) < half else z1_ref
            zc = (g * gw) % half
            gt = y * _silu(zr[:, zc:zc + gw].astype(F32))
            ms = jnp.mean(gt * gt, axis=-1, keepdims=True)
            g_ref[:, g * gw:(g + 1) * gw] = (gt * lax.rsqrt(ms + EPS) * ng_ref[:, g * gw:(g + 1) * gw]).astype(g_ref.dtype)

    for g0 in range(0, SSD_GROUPS, SSD_GROUP_BATCH):
        run_groups(range(g0, g0 + SSD_GROUP_BATCH))

    xext_ref[:, 0:8, :] = xext_ref[:, q_len:q_len + 8, :]

    @pl.when(c == pl.num_programs(1) - 1)
    def _():
        stout_ref[...] = state_ref[...]


def _ssd(zx, dtraw, conv_w, conv_b, dt_bias, a_log, d_exp, norm_g, st0, tail0, *, nb, nck,
         first_valid_row):
    d_inner = norm_g.shape[1]
    conv_dim = conv_w.shape[1]
    half = 2 * SSD_GROUPS * SSD_STATE
    assert d_inner == 2 * half and zx.shape[1] == d_inner + conv_dim and conv_dim == 3 * half
    gw = d_inner // SSD_GROUPS
    nh = dt_bias.shape[1]
    sel = ((jnp.arange(3 * nh)[:, None] % nh) == (jnp.arange(d_inner)[None, :] // SSD_HEADDIM)).astype(BF16)

    def rows(col):
        return pl.BlockSpec((CHUNK, half), lambda b, c: (b * nck + c, col))

    def full(a):
        nd = a.ndim
        return pl.BlockSpec(a.shape, lambda b, c: (0,) * nd)

    kern = functools.partial(_ssd_kernel, first_valid_row=first_valid_row)
    return pl.pallas_call(
        kern,
        grid=(nb, nck),
        in_specs=[rows(0), rows(1), rows(2), rows(3), rows(4),
                  pl.BlockSpec((CHUNK, LANES), lambda b, c: (b * nck + c, 0)),
                  full(conv_w), full(conv_b), full(dt_bias), full(a_log), full(d_exp), full(norm_g),
                  full(st0), full(tail0), full(sel)],
        out_specs=[pl.BlockSpec((CHUNK, d_inner), lambda b, c: (b * nck + c, 0)),
                   pl.BlockSpec((None, SSD_GROUPS, SSD_STATE, gw), lambda b, c: (b, 0, 0, 0))],
        out_shape=[jax.ShapeDtypeStruct((nb * nck * CHUNK, d_inner), BF16),
                   jax.ShapeDtypeStruct((nb, SSD_GROUPS, SSD_STATE, gw), F32)],
        scratch_shapes=[pltpu.VMEM((SSD_GROUPS, SSD_STATE, gw), F32),
                        pltpu.VMEM((conv_dim // LANES, 8 + CHUNK, LANES), F32)],
        compiler_params=_params(("parallel", "arbitrary"), 32),
        name="ssd_scan",
    )(zx, zx, zx, zx, zx, dtraw, conv_w, conv_b, dt_bias, a_log, d_exp, norm_g, st0, tail0, sel)


def _mm_res_kernel(a_ref, w_ref, h_ref, *rest, emit):
    o_ref = rest[-1] if emit != "h_and_u" else rest[-2]
    rows = a_ref.shape[0]
    sub = min(rows, 256)
    prods = [_dot(a_ref[r:r + sub, :], w_ref[...]) for r in range(0, rows, sub)]
    for idx, prod in enumerate(prods):
        rs = slice(idx * sub, (idx + 1) * sub)
        hn = h_ref[rs, :] + prod
        if emit == "h":
            o_ref[rs, :] = hn
            continue
        normed = hn * lax.rsqrt(jnp.mean(hn * hn, axis=-1, keepdims=True) + EPS) * rest[0][...]
        if emit == "normed":
            o_ref[rs, :] = normed
        else:
            o_ref[rs, :] = hn
            rest[-1][rs, :] = normed.astype(BF16)


def _mm_res(a, w, layer, h, tm, name, emit="h", gain=None):
    assert emit in ("h", "h_and_u", "normed") and (gain is None) == (emit == "h")
    m, k = a.shape
    n = w.shape[2]
    row_spec = pl.BlockSpec((tm, n), lambda i: (i, 0))
    in_specs = [pl.BlockSpec((tm, k), lambda i: (i, 0)),
                pl.BlockSpec((None, k, n), lambda i: (layer, 0, 0), pipeline_mode=pl.Buffered(1)),
                row_spec]
    args = [a, w, h]
    if gain is not None:
        in_specs.append(pl.BlockSpec((1, n), lambda i: (0, 0)))
        args.append(gain)
    out_specs = [row_spec]
    out_shape = [jax.ShapeDtypeStruct((m, n), F32)]
    if emit == "h_and_u":
        out_specs.append(row_spec)
        out_shape.append(jax.ShapeDtypeStruct((m, n), BF16))
    res = pl.pallas_call(
        functools.partial(_mm_res_kernel, emit=emit),
        grid=(m // tm,),
        in_specs=in_specs,
        out_specs=out_specs,
        out_shape=out_shape,
        compiler_params=_params(("parallel",), 56),
        name=name,
    )(*args)
    return res if emit == "h_and_u" else res[0]


def _gate_up_kernel(u_ref, wg_ref, wu_ref, o_ref):
    rows = u_ref.shape[0]
    sub = min(rows, GATE_UP_SUB_ROWS)
    parts = []
    for r in range(0, rows, sub):
        u = u_ref[r:r + sub, :]
        parts.append((_dot(u, wg_ref[...]), _dot(u, wu_ref[...])))
    for idx, (gate, up) in enumerate(parts):
        o_ref[idx * sub:(idx + 1) * sub, :] = (_silu(gate) * up).astype(o_ref.dtype)


def _gate_up(u, w_gate_up, layer, tm, name):
    m, d = u.shape
    f = w_gate_up.shape[2] // 2
    tf = FFN_HIDDEN_TILE
    nf = f // tf
    return pl.pallas_call(
        _gate_up_kernel,
        grid=(m // tm, nf),
        in_specs=[pl.BlockSpec((tm, d), lambda i, j: (i, 0)),
                  pl.BlockSpec((None, d, tf), lambda i, j: (layer, 0, j)),
                  pl.BlockSpec((None, d, tf), lambda i, j: (layer, 0, nf + j))],
        out_specs=pl.BlockSpec((tm, tf), lambda i, j: (i, j)),
        out_shape=jax.ShapeDtypeStruct((m, f), BF16),
        compiler_params=_params(("parallel", "arbitrary"), 48),
        name=name,
    )(u, w_gate_up, w_gate_up)


def _qkv_kernel(h_ref, gq_ref, gkv_ref, w_ref, o_ref, uq_ref, ukv_ref, *, nq_tiles):
    j = pl.program_id(1)

    @pl.when(j == 0)
    def _():
        rows = h_ref.shape[0]
        slab = min(rows, 256)
        for r in range(0, rows, slab):
            x = h_ref[r:r + slab, :]
            xn = x * lax.rsqrt(jnp.mean(x * x, axis=-1, keepdims=True) + EPS)
            uq_ref[r:r + slab, :] = (xn * gq_ref[...]).astype(BF16)
            ukv_ref[r:r + slab, :] = (xn * gkv_ref[...]).astype(BF16)

    @pl.when(j < nq_tiles)
    def _():
        o_ref[...] = _dot(uq_ref[...], w_ref[...]).astype(o_ref.dtype)

    @pl.when(j >= nq_tiles)
    def _():
        o_ref[...] = _dot(ukv_ref[...], w_ref[...]).astype(o_ref.dtype)


def _qkv(h, gq, gkv, w_qkv, nq_cols, tm):
    m, d = h.shape
    n = w_qkv.shape[1]
    tn = PROJ_COL_TILE
    kern = functools.partial(_qkv_kernel, nq_tiles=nq_cols // tn)
    return pl.pallas_call(
        kern,
        grid=(m // tm, n // tn),
        in_specs=[pl.BlockSpec((tm, d), lambda i, j: (i, 0)),
                  pl.BlockSpec((1, d), lambda i, j: (0, 0)),
                  pl.BlockSpec((1, d), lambda i, j: (0, 0)),
                  pl.BlockSpec((d, tn), lambda i, j: (0, j))],
        out_specs=pl.BlockSpec((tm, tn), lambda i, j: (i, j)),
        out_shape=jax.ShapeDtypeStruct((m, n), BF16),
        scratch_shapes=[pltpu.VMEM((tm, d), BF16), pltpu.VMEM((tm, d), BF16)],
        compiler_params=_params(("parallel", "arbitrary"), 48),
        name="qkv_proj",
    )(h, gq, gkv, w_qkv)


def _attn_kernel(q_ref, k_ref, v_ref, km_ref, vm_ref, o_ref, acc_ref, c_ref, *, scale):
    t = q_ref.shape[0]
    hd = SB_HEAD_DIM
    n_par = q_ref.shape[1] // hd
    qi = pl.program_id(2)
    ri = lax.broadcasted_iota(jnp.int32, (t, t), 0)
    ci = lax.broadcasted_iota(jnp.int32, (t, t), 1)
    upper = (ri > ci).astype(BF16)
    upper2 = jnp.concatenate([upper, upper], axis=0)

    def step(k_of, v_of, mask, u2):
        heads = [slice(g * hd, (g + 1) * hd) for g in range(n_par)]
        zs = [_dot_nt(q_ref[:, cols], k_of(cols)) * (scale * LOG2E) for cols in heads]
        ls2s, lk2s, hilos = [], [], []
        for z2 in zs:
            ls2 = jnp.minimum(z2, 0.0) - jnp.log(1.0 + jnp.exp2(-jnp.abs(z2))) * LOG2E
            lk2 = ls2 - z2
            if mask is not None:
                lk2 = jnp.where(mask, lk2, 0.0)
            hi = lk2.astype(BF16)
            lo = (lk2 - hi.astype(F32)).astype(BF16)
            ls2s.append(ls2)
            lk2s.append(lk2)
            hilos.append(jnp.concatenate([hi, lo], axis=1))
        tails = [_dot(hl, u2) for hl in hilos]
        cmax = None
        ps = []
        for g in range(n_par):
            c = c_ref[g]
            p = jnp.exp2(ls2s[g] + tails[g] + c)
            if mask is not None:
                p = jnp.where(mask, p, 0.0)
            ps.append(p.astype(BF16))
            c_new = c + jnp.sum(lk2s[g], axis=1, keepdims=True)
            c_ref[g] = c_new
            cmax = c_new if cmax is None else jnp.maximum(cmax, c_new)
        for g in range(n_par):
            acc_ref[g] += _dot(ps[g], v_of(heads[g]))
        return (jnp.max(cmax) > EXP_ZERO_F32 * LOG2E).astype(jnp.int32)

    acc_ref[...] = jnp.zeros_like(acc_ref)
    c_ref[...] = jnp.zeros_like(c_ref)

    def block(kb):
        s = pl.multiple_of(kb * t, t)
        return (lambda cols: k_ref[pl.ds(s, t), cols]), (lambda cols: v_ref[pl.ds(s, t), cols])

    go = step(*block(qi), ci < ri, upper2)

    def cond(carry):
        kb, alive = carry
        return jnp.logical_and(kb >= 0, alive > 0)

    def body(carry):
        kb, _ = carry
        return kb - 1, step(*block(kb), None, upper2)

    _, go = lax.while_loop(cond, body, (qi - 1, go))

    @pl.when(go > 0)
    def _():
        tm = km_ref.shape[0]
        mi = lax.broadcasted_iota(jnp.int32, (t, tm), 1)
        mr = lax.broadcasted_iota(jnp.int32, (tm, tm), 0)
        mc = lax.broadcasted_iota(jnp.int32, (tm, tm), 1)
        um = (mr > mc).astype(BF16)
        step(lambda cols: km_ref[:, cols], lambda cols: vm_ref[:, cols], mi < N_META,
             jnp.concatenate([um, um], axis=0))

    for g in range(n_par):
        o_ref[:, g * hd:(g + 1) * hd] = acc_ref[g].astype(o_ref.dtype)


def _attention(qkv, kv_meta, nb, seq, n_heads):
    t = ATTN_BLOCK
    n_par = ATTN_HEADS_PER_STEP
    w = n_par * SB_HEAD_DIM
    ng = n_heads // n_par
    kern = functools.partial(_attn_kernel, scale=SB_HEAD_DIM ** -0.5)
    qkv3 = qkv.reshape(nb, seq, qkv.shape[1])
    return pl.pallas_call(
        kern,
        grid=(nb, ng, seq // t),
        in_specs=[pl.BlockSpec((None, t, w), lambda b, h, i: (b, i, h)),
                  pl.BlockSpec((None, seq, w), lambda b, h, i: (b, 0, ng + h)),
                  pl.BlockSpec((None, seq, w), lambda b, h, i: (b, 0, 2 * ng + h)),
                  pl.BlockSpec((LANES, w), lambda b, h, i: (0, ng + h)),
                  pl.BlockSpec((LANES, w), lambda b, h, i: (0, 2 * ng + h))],
        out_specs=pl.BlockSpec((None, t, w), lambda b, h, i: (b, i, h)),
        out_shape=jax.ShapeDtypeStruct((nb, seq, n_heads * SB_HEAD_DIM), BF16),
        scratch_shapes=[pltpu.VMEM((n_par, t, SB_HEAD_DIM), F32), pltpu.VMEM((n_par, t, 1), F32)],
        compiler_params=_params(("parallel", "parallel", "arbitrary"), 48),
        name="sb_attention",
    )(qkv3, qkv3, qkv3, kv_meta, kv_meta)


def _row_tile(m):
    return 1024 if m % 1024 == 0 else m


def kernel(x, meta_tokens, norm_mix, norm_ffn, ssd_in_proj, ssd_conv_w, ssd_conv_b, ssd_dt_bias,
           ssd_a_log, ssd_d, ssd_norm, ssd_out_proj, kv_norm, w_kv, sb_w_q, sb_w_o, ffn_gate_up,
           ffn_down, final_norm):
    nb, seq, d = x.shape
    d_inner = ssd_out_proj.shape[1]
    conv_dim = ssd_conv_w.shape[2]
    n_ssd_heads = ssd_dt_bias.shape[1]
    sb_width = sb_w_q.shape[2]
    n_sb_heads = sb_width // SB_HEAD_DIM
    assert seq % ATTN_BLOCK == 0 and seq % CHUNK == 0 and n_ssd_heads <= LANES
    assert meta_tokens.shape[0] == N_META and N_META >= 8

    w_in = ssd_in_proj[:1].astype(BF16)
    w_dt = jnp.pad(ssd_in_proj[0][:, d_inner + conv_dim:], ((0, 0), (0, LANES - n_ssd_heads))).astype(BF16)
    w_out = ssd_out_proj.astype(BF16)
    w_qkv = jnp.concatenate([sb_w_q[0].astype(BF16), w_kv.astype(BF16)], axis=1)
    w_o = sb_w_o.astype(BF16)
    w_gu = ffn_gate_up.astype(BF16)
    w_dn = ffn_down.astype(BF16)
    row = lambda v: v.reshape(1, -1).astype(F32)
    d_exp = row(jnp.repeat(ssd_d[0], SSD_HEADDIM))
    conv_w = ssd_conv_w[0].astype(F32)
    conv_b = row(ssd_conv_b[0])
    ssd_consts = (conv_w, conv_b, row(ssd_dt_bias[0]), row(ssd_a_log[0]), d_exp, row(ssd_norm[0]))
    gw = d_inner // SSD_GROUPS
    n_ct = conv_dim // LANES

    hm = meta_tokens.astype(F32)
    zx_m, dt_m = _in_proj(hm, row(norm_mix[0]), w_in, w_dt, N_META)
    lead = CHUNK - N_META
    g_m, st_m = _ssd(jnp.pad(zx_m, ((lead, 0), (0, 0))), jnp.pad(dt_m, ((lead, 0), (0, 0))),
                     *ssd_consts, jnp.zeros((SSD_GROUPS, SSD_STATE, gw), F32),
                     jnp.zeros((n_ct, 8, LANES), F32), nb=1, nck=1, first_valid_row=lead)
    hm, um = _mm_res(g_m[lead:], w_out, 0, hm, N_META, "out_proj_meta", "h_and_u", row(norm_ffn[0]))
    hm = _mm_res(_gate_up(um, w_gu, 0, N_META, "ffn0_up_meta"), w_dn, 0, hm, N_META, "ffn0_down_meta")
    qkv_m = _qkv(hm, row(norm_mix[1]), row(kv_norm), w_qkv, sb_width, N_META)
    kv_meta = jnp.pad(qkv_m, ((0, LANES - N_META), (0, 0)))

    tm = _row_tile(nb * seq)
    assert seq % tm == 0
    h = x.reshape(nb * seq, d).astype(F32)
    zx, dtr = _in_proj(h, row(norm_mix[0]), w_in, w_dt, tm)
    tail0 = zx_m[N_META - 8:, d_inner:].astype(F32).reshape(8, n_ct, LANES).transpose(1, 0, 2)
    g, _ = _ssd(zx, dtr, *ssd_consts, st_m[0], tail0, nb=nb, nck=seq // CHUNK, first_valid_row=0)
    tm_up = 2 * tm if (nb * seq) % (2 * tm) == 0 else tm
    h, u = _mm_res(g, w_out, 0, h, tm // 4, "out_proj", "h_and_u", row(norm_ffn[0]))
    h = _mm_res(_gate_up(u, w_gu, 0, tm_up, "ffn0_up"), w_dn, 0, h, tm // 4, "ffn0_down")

    qkv = _qkv(h, row(norm_mix[1]), row(kv_norm), w_qkv, sb_width, tm)
    o = _attention(qkv, kv_meta, nb, seq, n_sb_heads)
    h, u = _mm_res(o.reshape(nb * seq, sb_width), w_o, 0, h, tm // 2, "o_proj", "h_and_u", row(norm_ffn[1]))
    out = _mm_res(_gate_up(u, w_gu, 1, tm_up, "ffn1_up"), w_dn, 1, h, tm // 4, "ffn1_down", "normed",
                  row(final_norm))
    return out.reshape(nb, seq, d)
```

```python
import functools

import jax
import jax.numpy as jnp
from jax import lax
from jax.experimental import pallas as pl
from jax.experimental.pallas import tpu as pltpu

F32 = jnp.float32
BF16 = jnp.bfloat16
EPS = 1e-6
CHUNK = 64
N_META = 16
SSD_HEADDIM = 64
SSD_GROUPS = 8
SSD_STATE = 128
SSD_CONV = 4
SSD_GROUP_BATCH = 4
FFN_HIDDEN_TILE = 512
GATE_UP_SUB_ROWS = 512
PROJ_COL_TILE = 1024
SB_HEAD_DIM = 128
LANES = 128
ATTN_BLOCK = 256
ATTN_HEADS_PER_STEP = 4
LOG2E = 1.4426950408889634
EXP_ZERO_F32 = -104.0
MIB = 1024 * 1024
HIGHEST = lax.Precision.HIGHEST


def _params(sem, vmem_mib):
    return pltpu.CompilerParams(dimension_semantics=sem, vmem_limit_bytes=vmem_mib * MIB)


def _dot(a, b):
    return jnp.dot(a, b, preferred_element_type=F32)


def _dot_nt(a, b):
    return lax.dot_general(a, b, (((1,), (1,)), ((), ())), preferred_element_type=F32)


def _dot_tn(a, b, precision=None):
    return lax.dot_general(a, b, (((0,), (0,)), ((), ())), precision=precision,
                           preferred_element_type=F32)


def _silu(x):
    hx = 0.5 * x
    return hx + hx * jnp.tanh(hx)


def _store_normed(x_ref, g_ref, u_ref):
    rows = x_ref.shape[0]
    slab = min(rows, 256)
    for r in range(0, rows, slab):
        x = x_ref[r:r + slab, :]
        ms = jnp.mean(x * x, axis=-1, keepdims=True)
        u_ref[r:r + slab, :] = (x * lax.rsqrt(ms + EPS) * g_ref[...]).astype(BF16)


def _in_proj_kernel(x_ref, g_ref, w_ref, wdt_ref, o_ref, odt_ref, u_ref):
    @pl.when(pl.program_id(1) == 0)
    def _():
        _store_normed(x_ref, g_ref, u_ref)
        odt_ref[...] = _dot(u_ref[...], wdt_ref[...])

    o_ref[...] = _dot(u_ref[...], w_ref[...]).astype(o_ref.dtype)


def _in_proj(x2d, gain, w_in, w_dt, tm):
    m, k = x2d.shape
    tn = PROJ_COL_TILE
    n = (w_in.shape[2] // tn) * tn
    return pl.pallas_call(
        _in_proj_kernel,
        grid=(m // tm, n // tn),
        in_specs=[
            pl.BlockSpec((tm, k), lambda i, j: (i, 0)),
            pl.BlockSpec((1, k), lambda i, j: (0, 0)),
            pl.BlockSpec((None, k, tn), lambda i, j: (0, 0, j)),
            pl.BlockSpec((k, LANES), lambda i, j: (0, 0)),
        ],
        out_specs=[
            pl.BlockSpec((tm, tn), lambda i, j: (i, j)),
            pl.BlockSpec((tm, LANES), lambda i, j: (i, 0)),
        ],
        out_shape=[jax.ShapeDtypeStruct((m, n), BF16), jax.ShapeDtypeStruct((m, LANES), F32)],
        scratch_shapes=[pltpu.VMEM((tm, k), BF16)],
        compiler_params=_params(("parallel", "arbitrary"), 48),
        name="in_proj",
    )(x2d, gain, w_in, w_dt)


def _ssd_kernel(z0_ref, z1_ref, xa_ref, xb_ref, bc_ref, dt_ref, cw_ref, cb_ref, dtb_ref,
                alog_ref, dsk_ref, ng_ref, st0_ref, tail0_ref, sel_ref,
                g_ref, stout_ref, state_ref, xext_ref, *, first_valid_row):
    c = pl.program_id(1)
    q_len = CHUNK
    d_inner = g_ref.shape[1]
    gw = d_inner // SSD_GROUPS
    half = xa_ref.shape[1]
    tiles_per_piece = half // LANES
    b_tile0 = d_inner // LANES
    c_tile0 = b_tile0 + SSD_GROUPS * SSD_STATE // LANES

    @pl.when(c == 0)
    def _():
        state_ref[...] = st0_ref[...]
        xext_ref[:, 0:8, :] = tail0_ref[...]

    for piece, ref in enumerate((xa_ref, xb_ref, bc_ref)):
        for t in range(tiles_per_piece):
            xext_ref[piece * tiles_per_piece + t, 8:8 + q_len, :] = ref[:, t * LANES:(t + 1) * LANES].astype(F32)

    row = lax.broadcasted_iota(jnp.int32, (q_len, 1), 0)
    valid = row >= first_valid_row

    def conv_act(t):
        cols = slice(t * LANES, (t + 1) * LANES)
        acc = cb_ref[:, cols]
        for k in range(SSD_CONV):
            s = 8 - (SSD_CONV - 1) + k
            acc = acc + cw_ref[k:k + 1, cols] * xext_ref[t, s:s + q_len, :]
        a = _silu(acc)
        return jnp.where(valid, a, 0.0) if first_valid_row else a

    nh = dtb_ref.shape[1]
    dtr = dt_ref[:, 0:nh] + dtb_ref[...]
    dt = jnp.maximum(dtr, 0.0) + jnp.log1p(jnp.exp(-jnp.abs(dtr)))
    if first_valid_row:
        dt = jnp.where(valid, dt, 0.0)
    dta = dt * (-jnp.exp(alog_ref[...]))

    def split3(a):
        hi = a.astype(BF16)
        r = a - hi.astype(F32)
        mid = r.astype(BF16)
        return hi, mid, (r - mid.astype(F32)).astype(BF16)

    ri = lax.broadcasted_iota(jnp.int32, (q_len, q_len), 0)
    ci = lax.broadcasted_iota(jnp.int32, (q_len, q_len), 1)
    rep3 = lambda m: jnp.concatenate([m.astype(BF16)] * 3, axis=0)
    dta3 = jnp.concatenate(split3(dta), axis=0)
    cum = _dot_tn(rep3(ri <= ci), dta3)
    cum_t = _dot_tn(dta3, rep3(ri <= ci))
    dt_t = _dot_tn(jnp.concatenate(split3(dt), axis=0), rep3(ri == ci))
    w_in = jnp.exp(cum[q_len - 1:q_len, :] - cum) * dt

    lane = lax.broadcasted_iota(jnp.int32, (q_len, LANES), 1)
    rowq = lax.broadcasted_iota(jnp.int32, (q_len, LANES), 0)
    left = lane < SSD_HEADDIM
    causal2 = rowq >= jnp.where(left, lane, lane - SSD_HEADDIM)
    rr = lax.broadcasted_iota(jnp.int32, (2 * q_len, LANES), 0)
    ll = lax.broadcasted_iota(jnp.int32, (2 * q_len, LANES), 1)
    blockdiag = (rr < q_len) == (ll < SSD_HEADDIM)

    w16 = w_in.astype(BF16)
    zero16 = jnp.zeros_like(w16)
    spread_lhs = jnp.concatenate(
        [jnp.concatenate(split3(cum), axis=1),
         jnp.concatenate([w16, zero16, zero16], axis=1)], axis=0)

    def pair_rows(a, h):
        return jnp.concatenate([a[h:h + 1, :], a[h + 1:h + 2, :]], axis=1)

    heads_per_group = gw // SSD_HEADDIM
    pairs = heads_per_group // 2

    def run_groups(groups):
        bms, cb2s, y_offs, spreads, xs = {}, {}, {}, {}, {}
        for g in groups:
            bm = conv_act(b_tile0 + g)
            cm16 = conv_act(c_tile0 + g).astype(BF16)
            bms[g] = bm
            for p in range(pairs):
                xs[g, p] = conv_act(g * pairs + p)
            cb2s[g] = _dot_nt(cm16, jnp.concatenate([bm, bm], axis=0).astype(BF16))
            y_offs[g] = _dot(cm16, state_ref[g].astype(BF16))
            spreads[g] = _dot(spread_lhs, sel_ref[:, g * gw:(g + 1) * gw])

        m2s, xbds, ea2s = {}, {}, {}
        for g in groups:
            for p in range(pairs):
                h = g * heads_per_group + 2 * p
                a2 = spreads[g][0:q_len, p * LANES:(p + 1) * LANES]
                seg = a2 - pair_rows(cum_t, h)
                m2 = cb2s[g] * jnp.exp(jnp.where(causal2, seg, -jnp.inf)) * pair_rows(dt_t, h)
                xs2 = xs[g, p]
                m2s[g, p] = m2.astype(BF16)
                xbds[g, p] = jnp.where(blockdiag, jnp.concatenate([xs2, xs2], axis=0), 0.0).astype(BF16)
                ea2s[g, p] = jnp.exp(a2)

        y_diags = {gp: _dot(m2s[gp], xbds[gp]) for gp in m2s}

        for g in groups:
            ys, xws, decs = [], [], []
            for p in range(pairs):
                sl = slice(p * LANES, (p + 1) * LANES)
                xs2 = xs[g, p]
                ys.append(y_diags[g, p] + y_offs[g][:, sl] * ea2s[g, p]
                          + xs2 * dsk_ref[:, g * gw + p * LANES:g * gw + (p + 1) * LANES])
                xws.append((xs2 * spreads[g][q_len:2 * q_len, sl]).astype(BF16))
                decs.append(ea2s[g, p][q_len - 1:q_len, :])
            xw = jnp.concatenate(xws, axis=1)
            dec = jnp.concatenate(decs, axis=1)
            state_ref[g] = state_ref[g] * dec + _dot(bms[g].T.astype(BF16), xw)
            y = jnp.concatenate(ys, axis=1)
            zr = z0_ref if (g * gw) < half else z1_ref
            zc = (g * gw) % half
            gt = y * _silu(zr[:, zc:zc + gw].astype(F32))
            ms = jnp.mean(gt * gt, axis=-1, keepdims=True)
            g_ref[:, g * gw:(g + 1) * gw] = (gt * lax.rsqrt(ms + EPS) * ng_ref[:, g * gw:(g + 1) * gw]).astype(g_ref.dtype)

    for g0 in range(0, SSD_GROUPS, SSD_GROUP_BATCH):
        run_groups(range(g0, g0 + SSD_GROUP_BATCH))

    xext_ref[:, 0:8, :] = xext_ref[:, q_len:q_len + 8, :]

    @pl.when(c == pl.num_programs(1) - 1)
    def _():
        stout_ref[...] = state_ref[...]


def _ssd(zx, dtraw, conv_w, conv_b, dt_bias, a_log, d_exp, norm_g, st0, tail0, *, nb, nck,
         first_valid_row):
    d_inner = norm_g.shape[1]
    conv_dim = conv_w.shape[1]
    half = 2 * SSD_GROUPS * SSD_STATE
    assert d_inner == 2 * half and zx.shape[1] == d_inner + conv_dim and conv_dim == 3 * half
    gw = d_inner // SSD_GROUPS
    nh = dt_bias.shape[1]
    sel = ((jnp.arange(3 * nh)[:, None] % nh) == (jnp.arange(d_inner)[None, :] // SSD_HEADDIM)).astype(BF16)

    def rows(col):
        return pl.BlockSpec((CHUNK, half), lambda b, c: (b * nck + c, col))

    def full(a):
        nd = a.ndim
        return pl.BlockSpec(a.shape, lambda b, c: (0,) * nd)

    kern = functools.partial(_ssd_kernel, first_valid_row=first_valid_row)
    return pl.pallas_call(
        kern,
        grid=(nb, nck),
        in_specs=[rows(0), rows(1), rows(2), rows(3), rows(4),
                  pl.BlockSpec((CHUNK, LANES), lambda b, c: (b * nck + c, 0)),
                  full(conv_w), full(conv_b), full(dt_bias), full(a_log), full(d_exp), full(norm_g),
                  full(st0), full(tail0), full(sel)],
        out_specs=[pl.BlockSpec((CHUNK, d_inner), lambda b, c: (b * nck + c, 0)),
                   pl.BlockSpec((None, SSD_GROUPS, SSD_STATE, gw), lambda b, c: (b, 0, 0, 0))],
        out_shape=[jax.ShapeDtypeStruct((nb * nck * CHUNK, d_inner), BF16),
                   jax.ShapeDtypeStruct((nb, SSD_GROUPS, SSD_STATE, gw), F32)],
        scratch_shapes=[pltpu.VMEM((SSD_GROUPS, SSD_STATE, gw), F32),
                        pltpu.VMEM((conv_dim // LANES, 8 + CHUNK, LANES), F32)],
        compiler_params=_params(("parallel", "arbitrary"), 32),
        name="ssd_scan",
    )(zx, zx, zx, zx, zx, dtraw, conv_w, conv_b, dt_bias, a_log, d_exp, norm_g, st0, tail0, sel)


def _mm_res_kernel(a_ref, w_ref, h_ref, *rest, emit, n_gains):
    gain_refs = rest[:n_gains]
    o_ref = rest[n_gains]
    u_refs = rest[n_gains + 1:]
    rows = a_ref.shape[0]
    sub = min(rows, 256)
    prods = [_dot(a_ref[r:r + sub, :], w_ref[...]) for r in range(0, rows, sub)]
    for idx, prod in enumerate(prods):
        rs = slice(idx * sub, (idx + 1) * sub)
        hn = h_ref[rs, :] + prod
        if emit == "h":
            o_ref[rs, :] = hn
            continue
        xn = hn * lax.rsqrt(jnp.mean(hn * hn, axis=-1, keepdims=True) + EPS)
        if emit == "normed":
            o_ref[rs, :] = xn * gain_refs[0][...]
        else:
            o_ref[rs, :] = hn
            for g_ref, u_ref in zip(gain_refs, u_refs):
                u_ref[rs, :] = (xn * g_ref[...]).astype(BF16)


def _mm_res(a, w, layer, h, tm, name, emit="h", gains=()):
    assert emit in ("h", "h_and_u", "normed")
    assert len(gains) == {"h": 0, "normed": 1}.get(emit, len(gains)) and (emit != "h_and_u" or gains)
    m, k = a.shape
    n = w.shape[2]
    row_spec = pl.BlockSpec((tm, n), lambda i: (i, 0))
    in_specs = [pl.BlockSpec((tm, k), lambda i: (i, 0)),
                pl.BlockSpec((None, k, n), lambda i: (layer, 0, 0), pipeline_mode=pl.Buffered(1)),
                row_spec] + [pl.BlockSpec((1, n), lambda i: (0, 0))] * len(gains)
    n_u = len(gains) if emit == "h_and_u" else 0
    res = pl.pallas_call(
        functools.partial(_mm_res_kernel, emit=emit, n_gains=len(gains)),
        grid=(m // tm,),
        in_specs=in_specs,
        out_specs=[row_spec] * (1 + n_u),
        out_shape=[jax.ShapeDtypeStruct((m, n), F32)] + [jax.ShapeDtypeStruct((m, n), BF16)] * n_u,
        compiler_params=_params(("parallel",), 56),
        name=name,
    )(a, w, h, *gains)
    return res if n_u else res[0]


def _gate_up_kernel(u_ref, wg_ref, wu_ref, o_ref):
    rows = u_ref.shape[0]
    sub = min(rows, GATE_UP_SUB_ROWS)
    parts = []
    for r in range(0, rows, sub):
        u = u_ref[r:r + sub, :]
        parts.append((_dot(u, wg_ref[...]), _dot(u, wu_ref[...])))
    for idx, (gate, up) in enumerate(parts):
        o_ref[idx * sub:(idx + 1) * sub, :] = (_silu(gate) * up).astype(o_ref.dtype)


def _gate_up(u, w_gate_up, layer, tm, name):
    m, d = u.shape
    f = w_gate_up.shape[2] // 2
    tf = FFN_HIDDEN_TILE
    nf = f // tf
    return pl.pallas_call(
        _gate_up_kernel,
        grid=(m // tm, nf),
        in_specs=[pl.BlockSpec((tm, d), lambda i, j: (i, 0)),
                  pl.BlockSpec((None, d, tf), lambda i, j: (layer, 0, j)),
                  pl.BlockSpec((None, d, tf), lambda i, j: (layer, 0, nf + j))],
        out_specs=pl.BlockSpec((tm, tf), lambda i, j: (i, j)),
        out_shape=jax.ShapeDtypeStruct((m, f), BF16),
        compiler_params=_params(("parallel", "arbitrary"), 48),
        name=name,
    )(u, w_gate_up, w_gate_up)


def _qkv_kernel(uq_ref, ukv_ref, w_ref, o_ref, *, nq_tiles):
    j = pl.program_id(1)

    @pl.when(j < nq_tiles)
    def _():
        o_ref[...] = _dot(uq_ref[...], w_ref[...]).astype(o_ref.dtype)

    @pl.when(j >= nq_tiles)
    def _():
        o_ref[...] = _dot(ukv_ref[...], w_ref[...]).astype(o_ref.dtype)


def _qkv(uq, ukv, w_qkv, nq_cols, tm):
    m, d = uq.shape
    n = w_qkv.shape[1]
    tn = PROJ_COL_TILE
    kern = functools.partial(_qkv_kernel, nq_tiles=nq_cols // tn)
    return pl.pallas_call(
        kern,
        grid=(m // tm, n // tn),
        in_specs=[pl.BlockSpec((tm, d), lambda i, j: (i, 0)),
                  pl.BlockSpec((tm, d), lambda i, j: (i, 0)),
                  pl.BlockSpec((d, tn), lambda i, j: (0, j))],
        out_specs=pl.BlockSpec((tm, tn), lambda i, j: (i, j)),
        out_shape=jax.ShapeDtypeStruct((m, n), BF16),
        compiler_params=_params(("parallel", "arbitrary"), 48),
        name="qkv_proj",
    )(uq, ukv, w_qkv)


def _attn_kernel(q_ref, k_ref, v_ref, km_ref, vm_ref, o_ref, acc_ref, c_ref, *, scale):
    t = q_ref.shape[0]
    hd = SB_HEAD_DIM
    n_par = q_ref.shape[1] // hd
    qi = pl.program_id(2)
    ri = lax.broadcasted_iota(jnp.int32, (t, t), 0)
    ci = lax.broadcasted_iota(jnp.int32, (t, t), 1)
    upper = (ri > ci).astype(BF16)

    def step(k_of, v_of, mask, u):
        heads = [slice(g * hd, (g + 1) * hd) for g in range(n_par)]
        zs = [_dot_nt(q_ref[:, cols], k_of(cols)) * (scale * LOG2E) for cols in heads]
        ls2s, lk2s = [], []
        for z2 in zs:
            ls2 = jnp.minimum(z2, 0.0) - jnp.log(1.0 + jnp.exp2(-jnp.abs(z2))) * LOG2E
            lk2 = ls2 - z2
            if mask is not None:
                lk2 = jnp.where(mask, lk2, 0.0)
            ls2s.append(ls2)
            lk2s.append(lk2)
        tails = [_dot(lk2.astype(BF16), u) for lk2 in lk2s]
        cmax = None
        ps = []
        for g in range(n_par):
            c = c_ref[g]
            p = jnp.exp2(ls2s[g] + tails[g] + c)
            if mask is not None:
                p = jnp.where(mask, p, 0.0)
            ps.append(p.astype(BF16))
            c_new = c + jnp.sum(lk2s[g], axis=1, keepdims=True)
            c_ref[g] = c_new
            cmax = c_new if cmax is None else jnp.maximum(cmax, c_new)
        for g in range(n_par):
            acc_ref[g] += _dot(ps[g], v_of(heads[g]))
        return (jnp.max(cmax) > EXP_ZERO_F32 * LOG2E).astype(jnp.int32)

    acc_ref[...] = jnp.zeros_like(acc_ref)
    c_ref[...] = jnp.zeros_like(c_ref)

    def block(kb):
        s = pl.multiple_of(kb * t, t)
        return (lambda cols: k_ref[pl.ds(s, t), cols]), (lambda cols: v_ref[pl.ds(s, t), cols])

    go = step(*block(qi), ci < ri, upper)

    def cond(carry):
        kb, alive = carry
        return jnp.logical_and(kb >= 0, alive > 0)

    def body(carry):
        kb, _ = carry
        return kb - 1, step(*block(kb), None, upper)

    _, go = lax.while_loop(cond, body, (qi - 1, go))

    @pl.when(go > 0)
    def _():
        tm = km_ref.shape[0]
        mi = lax.broadcasted_iota(jnp.int32, (t, tm), 1)
        mr = lax.broadcasted_iota(jnp.int32, (tm, tm), 0)
        mc = lax.broadcasted_iota(jnp.int32, (tm, tm), 1)
        step(lambda cols: km_ref[:, cols], lambda cols: vm_ref[:, cols], mi < N_META,
             (mr > mc).astype(BF16))

    for g in range(n_par):
        o_ref[:, g * hd:(g + 1) * hd] = acc_ref[g].astype(o_ref.dtype)


def _attention(qkv, kv_meta, nb, seq, n_heads):
    t = ATTN_BLOCK
    n_par = ATTN_HEADS_PER_STEP
    w = n_par * SB_HEAD_DIM
    ng = n_heads // n_par
    kern = functools.partial(_attn_kernel, scale=SB_HEAD_DIM ** -0.5)
    qkv3 = qkv.reshape(nb, seq, qkv.shape[1])
    return pl.pallas_call(
        kern,
        grid=(nb, ng, seq // t),
        in_specs=[pl.BlockSpec((None, t, w), lambda b, h, i: (b, i, h)),
                  pl.BlockSpec((None, seq, w), lambda b, h, i: (b, 0, ng + h)),
                  pl.BlockSpec((None, seq, w), lambda b, h, i: (b, 0, 2 * ng + h)),
                  pl.BlockSpec((LANES, w), lambda b, h, i: (0, ng + h)),
                  pl.BlockSpec((LANES, w), lambda b, h, i: (0, 2 * ng + h))],
        out_specs=pl.BlockSpec((None, t, w), lambda b, h, i: (b, i, h)),
        out_shape=jax.ShapeDtypeStruct((nb, seq, n_heads * SB_HEAD_DIM), BF16),
        scratch_shapes=[pltpu.VMEM((n_par, t, SB_HEAD_DIM), F32), pltpu.VMEM((n_par, t, 1), F32)],
        compiler_params=_params(("parallel", "parallel", "arbitrary"), 48),
        name="sb_attention",
    )(qkv3, qkv3, qkv3, kv_meta, kv_meta)


def _row_tile(m):
    return 1024 if m % 1024 == 0 else m


def kernel(x, meta_tokens, norm_mix, norm_ffn, ssd_in_proj, ssd_conv_w, ssd_conv_b, ssd_dt_bias,
           ssd_a_log, ssd_d, ssd_norm, ssd_out_proj, kv_norm, w_kv, sb_w_q, sb_w_o, ffn_gate_up,
           ffn_down, final_norm):
    nb, seq, d = x.shape
    d_inner = ssd_out_proj.shape[1]
    conv_dim = ssd_conv_w.shape[2]
    n_ssd_heads = ssd_dt_bias.shape[1]
    sb_width = sb_w_q.shape[2]
    n_sb_heads = sb_width // SB_HEAD_DIM
    assert seq % ATTN_BLOCK == 0 and seq % CHUNK == 0 and n_ssd_heads <= LANES
    assert meta_tokens.shape[0] == N_META and N_META >= 8

    w_in = ssd_in_proj[:1].astype(BF16)
    w_dt = jnp.pad(ssd_in_proj[0][:, d_inner + conv_dim:], ((0, 0), (0, LANES - n_ssd_heads))).astype(BF16)
    w_out = ssd_out_proj.astype(BF16)
    w_qkv = jnp.concatenate([sb_w_q[0].astype(BF16), w_kv.astype(BF16)], axis=1)
    w_o = sb_w_o.astype(BF16)
    w_gu = ffn_gate_up.astype(BF16)
    w_dn = ffn_down.astype(BF16)
    row = lambda v: v.reshape(1, -1).astype(F32)
    d_exp = row(jnp.repeat(ssd_d[0], SSD_HEADDIM))
    conv_w = ssd_conv_w[0].astype(F32)
    conv_b = row(ssd_conv_b[0])
    ssd_consts = (conv_w, conv_b, row(ssd_dt_bias[0]), row(ssd_a_log[0]), d_exp, row(ssd_norm[0]))
    gw = d_inner // SSD_GROUPS
    n_ct = conv_dim // LANES

    hm = meta_tokens.astype(F32)
    zx_m, dt_m = _in_proj(hm, row(norm_mix[0]), w_in, w_dt, N_META)
    lead = CHUNK - N_META
    g_m, st_m = _ssd(jnp.pad(zx_m, ((lead, 0), (0, 0))), jnp.pad(dt_m, ((lead, 0), (0, 0))),
                     *ssd_consts, jnp.zeros((SSD_GROUPS, SSD_STATE, gw), F32),
                     jnp.zeros((n_ct, 8, LANES), F32), nb=1, nck=1, first_valid_row=lead)
    qkv_gains = (row(norm_mix[1]), row(kv_norm))
    hm, um = _mm_res(g_m[lead:], w_out, 0, hm, N_META, "out_proj_meta", "h_and_u", (row(norm_ffn[0]),))
    _, uq_m, ukv_m = _mm_res(_gate_up(um, w_gu, 0, N_META, "ffn0_up_meta"), w_dn, 0, hm, N_META,
                             "ffn0_down_meta", "h_and_u", qkv_gains)
    qkv_m = _qkv(uq_m, ukv_m, w_qkv, sb_width, N_META)
    kv_meta = jnp.pad(qkv_m, ((0, LANES - N_META), (0, 0)))

    tm = _row_tile(nb * seq)
    assert seq % tm == 0
    h = x.reshape(nb * seq, d).astype(F32)
    zx, dtr = _in_proj(h, row(norm_mix[0]), w_in, w_dt, tm)
    tail0 = zx_m[N_META - 8:, d_inner:].astype(F32).reshape(8, n_ct, LANES).transpose(1, 0, 2)
    g, _ = _ssd(zx, dtr, *ssd_consts, st_m[0], tail0, nb=nb, nck=seq // CHUNK, first_valid_row=0)
    tm_up = 2 * tm if (nb * seq) % (2 * tm) == 0 else tm
    h, u = _mm_res(g, w_out, 0, h, tm // 4, "out_proj", "h_and_u", (row(norm_ffn[0]),))
    h, uq, ukv = _mm_res(_gate_up(u, w_gu, 0, tm_up, "ffn0_up"), w_dn, 0, h, tm // 4, "ffn0_down",
                         "h_and_u", qkv_gains)

    qkv = _qkv(uq, ukv, w_qkv, sb_width, tm)
    o = _attention(qkv, kv_meta, nb, seq, n_sb_heads)
    h, u = _mm_res(o.reshape(nb * seq, sb_width), w_o, 0, h, tm // 2, "o_proj", "h_and_u",
                   (row(norm_ffn[1]),))
    out = _mm_res(_gate_up(u, w_gu, 1, tm_up, "ffn1_up"), w_dn, 1, h, tm // 4, "ffn1_down", "normed",
                  (row(final_norm),))
    return out.reshape(nb, seq, d)
```

```python
import functools

import jax
import jax.numpy as jnp
from jax import lax
from jax.experimental import pallas as pl
from jax.experimental.pallas import tpu as pltpu

F32 = jnp.float32
BF16 = jnp.bfloat16
EPS = 1e-6
CHUNK = 64
N_META = 16
SSD_HEADDIM = 64
SSD_GROUPS = 8
SSD_STATE = 128
SSD_CONV = 4
SSD_GROUP_BATCH = 4
FFN_HIDDEN_TILE = 512
GATE_UP_SUB_ROWS = 512
PROJ_COL_TILE = 1024
SB_HEAD_DIM = 128
LANES = 128
ATTN_BLOCK = 256
ATTN_HEADS_PER_STEP = 4
LOG2E = 1.4426950408889634
EXP_ZERO_F32 = -104.0
MIB = 1024 * 1024
HIGHEST = lax.Precision.HIGHEST


def _params(sem, vmem_mib):
    return pltpu.CompilerParams(dimension_semantics=sem, vmem_limit_bytes=vmem_mib * MIB)


def _dot(a, b):
    return jnp.dot(a, b, preferred_element_type=F32)


def _dot_nt(a, b):
    return lax.dot_general(a, b, (((1,), (1,)), ((), ())), preferred_element_type=F32)


def _dot_tn(a, b, precision=None):
    return lax.dot_general(a, b, (((0,), (0,)), ((), ())), precision=precision,
                           preferred_element_type=F32)


def _silu(x):
    hx = 0.5 * x
    return hx + hx * jnp.tanh(hx)


def _store_normed(x_ref, g_ref, u_ref):
    rows = x_ref.shape[0]
    slab = min(rows, 256)
    for r in range(0, rows, slab):
        x = x_ref[r:r + slab, :]
        ms = jnp.mean(x * x, axis=-1, keepdims=True)
        u_ref[r:r + slab, :] = (x * lax.rsqrt(ms + EPS) * g_ref[...]).astype(BF16)


def _in_proj_kernel(x_ref, g_ref, w_ref, wdt_ref, o_ref, odt_ref, u_ref):
    @pl.when(pl.program_id(1) == 0)
    def _():
        _store_normed(x_ref, g_ref, u_ref)
        odt_ref[...] = _dot(u_ref[...], wdt_ref[...])

    o_ref[...] = _dot(u_ref[...], w_ref[...]).astype(o_ref.dtype)


def _in_proj(x2d, gain, w_in, w_dt, tm):
    m, k = x2d.shape
    tn = PROJ_COL_TILE
    n = (w_in.shape[2] // tn) * tn
    return pl.pallas_call(
        _in_proj_kernel,
        grid=(m // tm, n // tn),
        in_specs=[
            pl.BlockSpec((tm, k), lambda i, j: (i, 0)),
            pl.BlockSpec((1, k), lambda i, j: (0, 0)),
            pl.BlockSpec((None, k, tn), lambda i, j: (0, 0, j)),
            pl.BlockSpec((k, LANES), lambda i, j: (0, 0)),
        ],
        out_specs=[
            pl.BlockSpec((tm, tn), lambda i, j: (i, j)),
            pl.BlockSpec((tm, LANES), lambda i, j: (i, 0)),
        ],
        out_shape=[jax.ShapeDtypeStruct((m, n), BF16), jax.ShapeDtypeStruct((m, LANES), F32)],
        scratch_shapes=[pltpu.VMEM((tm, k), BF16)],
        compiler_params=_params(("parallel", "arbitrary"), 48),
        name="in_proj",
    )(x2d, gain, w_in, w_dt)


def _ssd_kernel(z0_ref, z1_ref, xa_ref, xb_ref, bc_ref, dt_ref, cw_ref, cb_ref, dtb_ref,
                alog_ref, dsk_ref, ng_ref, st0_ref, tail0_ref, sel_ref,
                g_ref, stout_ref, state_ref, xext_ref, *, first_valid_row):
    c = pl.program_id(1)
    q_len = CHUNK
    d_inner = g_ref.shape[1]
    gw = d_inner // SSD_GROUPS
    half = xa_ref.shape[1]
    tiles_per_piece = half // LANES
    b_tile0 = d_inner // LANES
    c_tile0 = b_tile0 + SSD_GROUPS * SSD_STATE // LANES

    @pl.when(c == 0)
    def _():
        state_ref[...] = st0_ref[...]
        xext_ref[:, 0:8, :] = tail0_ref[...]

    for piece, ref in enumerate((xa_ref, xb_ref, bc_ref)):
        for t in range(tiles_per_piece):
            xext_ref[piece * tiles_per_piece + t, 8:8 + q_len, :] = ref[:, t * LANES:(t + 1) * LANES].astype(F32)

    row = lax.broadcasted_iota(jnp.int32, (q_len, 1), 0)
    valid = row >= first_valid_row

    def conv_act(t):
        cols = slice(t * LANES, (t + 1) * LANES)
        acc = cb_ref[:, cols]
        for k in range(SSD_CONV):
            s = 8 - (SSD_CONV - 1) + k
            acc = acc + cw_ref[k:k + 1, cols] * xext_ref[t, s:s + q_len, :]
        a = _silu(acc)
        return jnp.where(valid, a, 0.0) if first_valid_row else a

    nh = dtb_ref.shape[1]
    dtr = dt_ref[:, 0:nh] + dtb_ref[...]
    dt = jnp.maximum(dtr, 0.0) + jnp.log1p(jnp.exp(-jnp.abs(dtr)))
    if first_valid_row:
        dt = jnp.where(valid, dt, 0.0)
    dta = dt * (-jnp.exp(alog_ref[...]))

    def split3(a):
        hi = a.astype(BF16)
        r = a - hi.astype(F32)
        mid = r.astype(BF16)
        return hi, mid, (r - mid.astype(F32)).astype(BF16)

    ri = lax.broadcasted_iota(jnp.int32, (q_len, q_len), 0)
    ci = lax.broadcasted_iota(jnp.int32, (q_len, q_len), 1)
    rep3 = lambda m: jnp.concatenate([m.astype(BF16)] * 3, axis=0)
    dta3 = jnp.concatenate(split3(dta), axis=0)
    cum = _dot_tn(rep3(ri <= ci), dta3)
    cum_t = _dot_tn(dta3, rep3(ri <= ci))
    dt_t = _dot_tn(jnp.concatenate(split3(dt), axis=0), rep3(ri == ci))
    w_in = jnp.exp(cum[q_len - 1:q_len, :] - cum) * dt

    lane = lax.broadcasted_iota(jnp.int32, (q_len, LANES), 1)
    rowq = lax.broadcasted_iota(jnp.int32, (q_len, LANES), 0)
    left = lane < SSD_HEADDIM
    causal2 = rowq >= jnp.where(left, lane, lane - SSD_HEADDIM)
    rr = lax.broadcasted_iota(jnp.int32, (2 * q_len, LANES), 0)
    ll = lax.broadcasted_iota(jnp.int32, (2 * q_len, LANES), 1)
    blockdiag = (rr < q_len) == (ll < SSD_HEADDIM)

    w16 = w_in.astype(BF16)
    zero16 = jnp.zeros_like(w16)
    spread_lhs = jnp.concatenate(
        [jnp.concatenate(split3(cum), axis=1),
         jnp.concatenate([w16, zero16, zero16], axis=1)], axis=0)

    def pair_rows(a, h):
        return jnp.concatenate([a[h:h + 1, :], a[h + 1:h + 2, :]], axis=1)

    heads_per_group = gw // SSD_HEADDIM
    pairs = heads_per_group // 2

    def run_groups(groups):
        bms, cb2s, y_offs, spreads, xs = {}, {}, {}, {}, {}
        for g in groups:
            bm = conv_act(b_tile0 + g)
            cm16 = conv_act(c_tile0 + g).astype(BF16)
            bms[g] = bm
            for p in range(pairs):
                xs[g, p] = conv_act(g * pairs + p)
            cb2s[g] = _dot_nt(cm16, jnp.concatenate([bm, bm], axis=0).astype(BF16))
            y_offs[g] = _dot(cm16, state_ref[g].astype(BF16))
            spreads[g] = _dot(spread_lhs, sel_ref[:, g * gw:(g + 1) * gw])

        m2s, xbds, ea2s = {}, {}, {}
        for g in groups:
            for p in range(pairs):
                h = g * heads_per_group + 2 * p
                a2 = spreads[g][0:q_len, p * LANES:(p + 1) * LANES]
                seg = a2 - pair_rows(cum_t, h)
                m2 = cb2s[g] * jnp.exp(jnp.where(causal2, seg, -jnp.inf)) * pair_rows(dt_t, h)
                xs2 = xs[g, p]
                m2s[g, p] = m2.astype(BF16)
                xbds[g, p] = jnp.where(blockdiag, jnp.concatenate([xs2, xs2], axis=0), 0.0).astype(BF16)
                ea2s[g, p] = jnp.exp(a2)

        y_diags = {gp: _dot(m2s[gp], xbds[gp]) for gp in m2s}

        for g in groups:
            ys, xws, decs = [], [], []
            for p in range(pairs):
                sl = slice(p * LANES, (p + 1) * LANES)
                xs2 = xs[g, p]
                ys.append(y_diags[g, p] + y_offs[g][:, sl] * ea2s[g, p]
                          + xs2 * dsk_ref[:, g * gw + p * LANES:g * gw + (p + 1) * LANES])
                xws.append((xs2 * spreads[g][q_len:2 * q_len, sl]).astype(BF16))
                decs.append(ea2s[g, p][q_len - 1:q_len, :])
            xw = jnp.concatenate(xws, axis=1)
            dec = jnp.concatenate(decs, axis=1)
            state_ref[g] = state_ref[g] * dec + _dot(bms[g].T.astype(BF16), xw)
            y = jnp.concatenate(ys, axis=1)
            zr = z0_ref if (g * gw) < half else z1_ref
            zc = (g * gw) % half
            gt = y * _silu(zr[:, zc:zc + gw].astype(F32))
            ms = jnp.mean(gt * gt, axis=-1, keepdims=True)
            g_ref[:, g * gw:(g + 1) * gw] = (gt * lax.rsqrt(ms + EPS) * ng_ref[:, g * gw:(g + 1) * gw]).astype(g_ref.dtype)

    for g0 in range(0, SSD_GROUPS, SSD_GROUP_BATCH):
        run_groups(range(g0, g0 + SSD_GROUP_BATCH))

    xext_ref[:, 0:8, :] = xext_ref[:, q_len:q_len + 8, :]

    @pl.when(c == pl.num_programs(1) - 1)
    def _():
        stout_ref[...] = state_ref[...]


def _ssd(zx, dtraw, conv_w, conv_b, dt_bias, a_log, d_exp, norm_g, st0, tail0, *, nb, nck,
         first_valid_row):
    d_inner = norm_g.shape[1]
    conv_dim = conv_w.shape[1]
    half = 2 * SSD_GROUPS * SSD_STATE
    assert d_inner == 2 * half and zx.shape[1] == d_inner + conv_dim and conv_dim == 3 * half
    gw = d_inner // SSD_GROUPS
    nh = dt_bias.shape[1]
    sel = ((jnp.arange(3 * nh)[:, None] % nh) == (jnp.arange(d_inner)[None, :] // SSD_HEADDIM)).astype(BF16)

    def rows(col):
        return pl.BlockSpec((CHUNK, half), lambda b, c: (b * nck + c, col))

    def full(a):
        nd = a.ndim
        return pl.BlockSpec(a.shape, lambda b, c: (0,) * nd)

    kern = functools.partial(_ssd_kernel, first_valid_row=first_valid_row)
    return pl.pallas_call(
        kern,
        grid=(nb, nck),
        in_specs=[rows(0), rows(1), rows(2), rows(3), rows(4),
                  pl.BlockSpec((CHUNK, LANES), lambda b, c: (b * nck + c, 0)),
                  full(conv_w), full(conv_b), full(dt_bias), full(a_log), full(d_exp), full(norm_g),
                  full(st0), full(tail0), full(sel)],
        out_specs=[pl.BlockSpec((CHUNK, d_inner), lambda b, c: (b * nck + c, 0)),
                   pl.BlockSpec((None, SSD_GROUPS, SSD_STATE, gw), lambda b, c: (b, 0, 0, 0))],
        out_shape=[jax.ShapeDtypeStruct((nb * nck * CHUNK, d_inner), BF16),
                   jax.ShapeDtypeStruct((nb, SSD_GROUPS, SSD_STATE, gw), F32)],
        scratch_shapes=[pltpu.VMEM((SSD_GROUPS, SSD_STATE, gw), F32),
                        pltpu.VMEM((conv_dim // LANES, 8 + CHUNK, LANES), F32)],
        compiler_params=_params(("parallel", "arbitrary"), 32),
        name="ssd_scan",
    )(zx, zx, zx, zx, zx, dtraw, conv_w, conv_b, dt_bias, a_log, d_exp, norm_g, st0, tail0, sel)


def _mm_res_kernel(a_ref, w_ref, h_ref, *rest, emit, n_gains):
    gain_refs = rest[:n_gains]
    o_ref = rest[n_gains]
    u_refs = rest[n_gains + 1:]
    rows = a_ref.shape[0]
    sub = min(rows, 256)
    prods = [_dot(a_ref[r:r + sub, :], w_ref[...]) for r in range(0, rows, sub)]
    for idx, prod in enumerate(prods):
        rs = slice(idx * sub, (idx + 1) * sub)
        hn = h_ref[rs, :] + prod
        if emit == "h":
            o_ref[rs, :] = hn
            continue
        xn = hn * lax.rsqrt(jnp.mean(hn * hn, axis=-1, keepdims=True) + EPS)
        if emit == "normed":
            o_ref[rs, :] = xn * gain_refs[0][...]
        else:
            o_ref[rs, :] = hn
            for g_ref, u_ref in zip(gain_refs, u_refs):
                u_ref[rs, :] = (xn * g_ref[...]).astype(BF16)


def _mm_res(a, w, layer, h, tm, name, emit="h", gains=()):
    assert emit in ("h", "h_and_u", "normed")
    assert len(gains) == {"h": 0, "normed": 1}.get(emit, len(gains)) and (emit != "h_and_u" or gains)
    m, k = a.shape
    n = w.shape[2]
    row_spec = pl.BlockSpec((tm, n), lambda i: (i, 0))
    in_specs = [pl.BlockSpec((tm, k), lambda i: (i, 0)),
                pl.BlockSpec((None, k, n), lambda i: (layer, 0, 0), pipeline_mode=pl.Buffered(1)),
                row_spec] + [pl.BlockSpec((1, n), lambda i: (0, 0))] * len(gains)
    n_u = len(gains) if emit == "h_and_u" else 0
    res = pl.pallas_call(
        functools.partial(_mm_res_kernel, emit=emit, n_gains=len(gains)),
        grid=(m // tm,),
        in_specs=in_specs,
        out_specs=[row_spec] * (1 + n_u),
        out_shape=[jax.ShapeDtypeStruct((m, n), F32)] + [jax.ShapeDtypeStruct((m, n), BF16)] * n_u,
        compiler_params=_params(("parallel",), 56),
        name=name,
    )(a, w, h, *gains)
    return res if n_u else res[0]


def _gate_up_kernel(u_ref, wg_ref, wu_ref, o_ref, wg16_ref, wu16_ref):
    @pl.when(pl.program_id(1) == 0)
    def _():
        wg16_ref[...] = wg_ref[...].astype(BF16)
        wu16_ref[...] = wu_ref[...].astype(BF16)

    rows = u_ref.shape[0]
    sub = min(rows, GATE_UP_SUB_ROWS)
    parts = []
    for r in range(0, rows, sub):
        u = u_ref[r:r + sub, :]
        parts.append((_dot(u, wg16_ref[...]), _dot(u, wu16_ref[...])))
    for idx, (gate, up) in enumerate(parts):
        o_ref[idx * sub:(idx + 1) * sub, :] = (_silu(gate) * up).astype(o_ref.dtype)


def _gate_up(u, w_gate_up, layer, tm, name):
    m, d = u.shape
    f = w_gate_up.shape[2] // 2
    tf = FFN_HIDDEN_TILE
    nf = f // tf
    return pl.pallas_call(
        _gate_up_kernel,
        grid=(nf, m // tm),
        in_specs=[pl.BlockSpec((tm, d), lambda j, i: (i, 0)),
                  pl.BlockSpec((None, d, tf), lambda j, i: (layer, 0, j)),
                  pl.BlockSpec((None, d, tf), lambda j, i: (layer, 0, nf + j))],
        out_specs=pl.BlockSpec((tm, tf), lambda j, i: (i, j)),
        out_shape=jax.ShapeDtypeStruct((m, f), BF16),
        scratch_shapes=[pltpu.VMEM((d, tf), BF16), pltpu.VMEM((d, tf), BF16)],
        compiler_params=_params(("arbitrary", "arbitrary"), 48),
        name=name,
    )(u, w_gate_up, w_gate_up)


def _qkv_kernel(uq_ref, ukv_ref, w_ref, o_ref, *, nq_tiles):
    j = pl.program_id(1)

    @pl.when(j < nq_tiles)
    def _():
        o_ref[...] = _dot(uq_ref[...], w_ref[...]).astype(o_ref.dtype)

    @pl.when(j >= nq_tiles)
    def _():
        o_ref[...] = _dot(ukv_ref[...], w_ref[...]).astype(o_ref.dtype)


def _qkv(uq, ukv, w_qkv, nq_cols, tm):
    m, d = uq.shape
    n = w_qkv.shape[1]
    tn = PROJ_COL_TILE
    kern = functools.partial(_qkv_kernel, nq_tiles=nq_cols // tn)
    return pl.pallas_call(
        kern,
        grid=(m // tm, n // tn),
        in_specs=[pl.BlockSpec((tm, d), lambda i, j: (i, 0)),
                  pl.BlockSpec((tm, d), lambda i, j: (i, 0)),
                  pl.BlockSpec((d, tn), lambda i, j: (0, j))],
        out_specs=pl.BlockSpec((tm, tn), lambda i, j: (i, j)),
        out_shape=jax.ShapeDtypeStruct((m, n), BF16),
        compiler_params=_params(("parallel", "arbitrary"), 48),
        name="qkv_proj",
    )(uq, ukv, w_qkv)


def _attn_kernel(q_ref, k_ref, v_ref, km_ref, vm_ref, o_ref, acc_ref, c_ref, *, scale):
    t = q_ref.shape[0]
    hd = SB_HEAD_DIM
    n_par = q_ref.shape[1] // hd
    qi = pl.program_id(2)
    ri = lax.broadcasted_iota(jnp.int32, (t, t), 0)
    ci = lax.broadcasted_iota(jnp.int32, (t, t), 1)
    upper = (ri > ci).astype(BF16)

    def step(k_of, v_of, mask, u):
        heads = [slice(g * hd, (g + 1) * hd) for g in range(n_par)]
        zs = [_dot_nt(q_ref[:, cols], k_of(cols)) * (scale * LOG2E) for cols in heads]
        ls2s, lk2s = [], []
        for z2 in zs:
            ls2 = jnp.minimum(z2, 0.0) - jnp.log(1.0 + jnp.exp2(-jnp.abs(z2))) * LOG2E
            lk2 = ls2 - z2
            if mask is not None:
                lk2 = jnp.where(mask, lk2, 0.0)
            ls2s.append(ls2)
            lk2s.append(lk2)
        tails = [_dot(lk2.astype(BF16), u) for lk2 in lk2s]
        cmax = None
        ps = []
        for g in range(n_par):
            c = c_ref[g]
            p = jnp.exp2(ls2s[g] + tails[g] + c)
            if mask is not None:
                p = jnp.where(mask, p, 0.0)
            ps.append(p.astype(BF16))
            c_new = c + jnp.sum(lk2s[g], axis=1, keepdims=True)
            c_ref[g] = c_new
            cmax = c_new if cmax is None else jnp.maximum(cmax, c_new)
        for g in range(n_par):
            acc_ref[g] += _dot(ps[g], v_of(heads[g]))
        return (jnp.max(cmax) > EXP_ZERO_F32 * LOG2E).astype(jnp.int32)

    acc_ref[...] = jnp.zeros_like(acc_ref)
    c_ref[...] = jnp.zeros_like(c_ref)

    def block(kb):
        s = pl.multiple_of(kb * t, t)
        return (lambda cols: k_ref[pl.ds(s, t), cols]), (lambda cols: v_ref[pl.ds(s, t), cols])

    go = step(*block(qi), ci < ri, upper)

    def cond(carry):
        kb, alive = carry
        return jnp.logical_and(kb >= 0, alive > 0)

    def body(carry):
        kb, _ = carry
        return kb - 1, step(*block(kb), None, upper)

    _, go = lax.while_loop(cond, body, (qi - 1, go))

    @pl.when(go > 0)
    def _():
        tm = km_ref.shape[0]
        mi = lax.broadcasted_iota(jnp.int32, (t, tm), 1)
        mr = lax.broadcasted_iota(jnp.int32, (tm, tm), 0)
        mc = lax.broadcasted_iota(jnp.int32, (tm, tm), 1)
        step(lambda cols: km_ref[:, cols], lambda cols: vm_ref[:, cols], mi < N_META,
             (mr > mc).astype(BF16))

    for g in range(n_par):
        o_ref[:, g * hd:(g + 1) * hd] = acc_ref[g].astype(o_ref.dtype)


def _attention(qkv, kv_meta, nb, seq, n_heads):
    t = ATTN_BLOCK
    n_par = ATTN_HEADS_PER_STEP
    w = n_par * SB_HEAD_DIM
    ng = n_heads // n_par
    kern = functools.partial(_attn_kernel, scale=SB_HEAD_DIM ** -0.5)
    qkv3 = qkv.reshape(nb, seq, qkv.shape[1])
    return pl.pallas_call(
        kern,
        grid=(nb, ng, seq // t),
        in_specs=[pl.BlockSpec((None, t, w), lambda b, h, i: (b, i, h)),
                  pl.BlockSpec((None, seq, w), lambda b, h, i: (b, 0, ng + h)),
                  pl.BlockSpec((None, seq, w), lambda b, h, i: (b, 0, 2 * ng + h)),
                  pl.BlockSpec((LANES, w), lambda b, h, i: (0, ng + h)),
                  pl.BlockSpec((LANES, w), lambda b, h, i: (0, 2 * ng + h))],
        out_specs=pl.BlockSpec((None, t, w), lambda b, h, i: (b, i, h)),
        out_shape=jax.ShapeDtypeStruct((nb, seq, n_heads * SB_HEAD_DIM), BF16),
        scratch_shapes=[pltpu.VMEM((n_par, t, SB_HEAD_DIM), F32), pltpu.VMEM((n_par, t, 1), F32)],
        compiler_params=_params(("parallel", "parallel", "arbitrary"), 48),
        name="sb_attention",
    )(qkv3, qkv3, qkv3, kv_meta, kv_meta)


def _row_tile(m):
    return 1024 if m % 1024 == 0 else m


def kernel(x, meta_tokens, norm_mix, norm_ffn, ssd_in_proj, ssd_conv_w, ssd_conv_b, ssd_dt_bias,
           ssd_a_log, ssd_d, ssd_norm, ssd_out_proj, kv_norm, w_kv, sb_w_q, sb_w_o, ffn_gate_up,
           ffn_down, final_norm):
    nb, seq, d = x.shape
    d_inner = ssd_out_proj.shape[1]
    conv_dim = ssd_conv_w.shape[2]
    n_ssd_heads = ssd_dt_bias.shape[1]
    sb_width = sb_w_q.shape[2]
    n_sb_heads = sb_width // SB_HEAD_DIM
    assert seq % ATTN_BLOCK == 0 and seq % CHUNK == 0 and n_ssd_heads <= LANES
    assert meta_tokens.shape[0] == N_META and N_META >= 8

    w_in = ssd_in_proj[:1].astype(BF16)
    w_dt = jnp.pad(ssd_in_proj[0][:, d_inner + conv_dim:], ((0, 0), (0, LANES - n_ssd_heads))).astype(BF16)
    w_out = ssd_out_proj.astype(BF16)
    w_qkv = jnp.concatenate([sb_w_q[0].astype(BF16), w_kv.astype(BF16)], axis=1)
    w_o = sb_w_o.astype(BF16)
    w_gu = ffn_gate_up.astype(F32)
    w_dn = ffn_down.astype(BF16)
    row = lambda v: v.reshape(1, -1).astype(F32)
    d_exp = row(jnp.repeat(ssd_d[0], SSD_HEADDIM))
    conv_w = ssd_conv_w[0].astype(F32)
    conv_b = row(ssd_conv_b[0])
    ssd_consts = (conv_w, conv_b, row(ssd_dt_bias[0]), row(ssd_a_log[0]), d_exp, row(ssd_norm[0]))
    gw = d_inner // SSD_GROUPS
    n_ct = conv_dim // LANES

    hm = meta_tokens.astype(F32)
    zx_m, dt_m = _in_proj(hm, row(norm_mix[0]), w_in, w_dt, N_META)
    lead = CHUNK - N_META
    g_m, st_m = _ssd(jnp.pad(zx_m, ((lead, 0), (0, 0))), jnp.pad(dt_m, ((lead, 0), (0, 0))),
                     *ssd_consts, jnp.zeros((SSD_GROUPS, SSD_STATE, gw), F32),
                     jnp.zeros((n_ct, 8, LANES), F32), nb=1, nck=1, first_valid_row=lead)
    qkv_gains = (row(norm_mix[1]), row(kv_norm))
    hm, um = _mm_res(g_m[lead:], w_out, 0, hm, N_META, "out_proj_meta", "h_and_u", (row(norm_ffn[0]),))
    _, uq_m, ukv_m = _mm_res(_gate_up(um, w_gu, 0, N_META, "ffn0_up_meta"), w_dn, 0, hm, N_META,
                             "ffn0_down_meta", "h_and_u", qkv_gains)
    qkv_m = _qkv(uq_m, ukv_m, w_qkv, sb_width, N_META)
    kv_meta = jnp.pad(qkv_m, ((0, LANES - N_META), (0, 0)))

    tm = _row_tile(nb * seq)
    assert seq % tm == 0
    h = x.reshape(nb * seq, d).astype(F32)
    zx, dtr = _in_proj(h, row(norm_mix[0]), w_in, w_dt, tm)
    tail0 = zx_m[N_META - 8:, d_inner:].astype(F32).reshape(8, n_ct, LANES).transpose(1, 0, 2)
    g, _ = _ssd(zx, dtr, *ssd_consts, st_m[0], tail0, nb=nb, nck=seq // CHUNK, first_valid_row=0)
    tm_up = 2 * tm if (nb * seq) % (2 * tm) == 0 else tm
    h, u = _mm_res(g, w_out, 0, h, tm // 4, "out_proj", "h_and_u", (row(norm_ffn[0]),))
    h, uq, ukv = _mm_res(_gate_up(u, w_gu, 0, tm_up, "ffn0_up"), w_dn, 0, h, tm // 4, "ffn0_down",
                         "h_and_u", qkv_gains)

    qkv = _qkv(uq, ukv, w_qkv, sb_width, tm)
    o = _attention(qkv, kv_meta, nb, seq, n_sb_heads)
    h, u = _mm_res(o.reshape(nb * seq, sb_width), w_o, 0, h, tm // 2, "o_proj", "h_and_u",
                   (row(norm_ffn[1]),))
    out = _mm_res(_gate_up(u, w_gu, 1, tm_up, "ffn1_up"), w_dn, 1, h, tm // 4, "ffn1_down", "normed",
                  (row(final_norm),))
    return out.reshape(nb, seq, d)
```
